```python
import jax, jax.numpy as jnp
from jax import lax
import numpy as np

D_MODEL = 1024
BATCH = 8
SEQ = 4096
DEPTH = 1
DEC_BATCH = 32
DEC_SEQ = 4
PAST_LEN = 16384
PAGE_SIZE = 128

HEAD_DIM = 64
N_GROUPS = 3
HEADS_PER_GROUP = 4
N_HEADS = N_GROUPS * HEADS_PER_GROUP
D_ATT = N_HEADS * HEAD_DIM
D_ATT_OUT = HEADS_PER_GROUP * HEAD_DIM
WINDOWS = (128, 512, 2048)
DILATIONS = (1, 4, 16)
C_CONV = D_MODEL
CONV_WIDTH = 31
D_FF = -(-8 * D_MODEL // (3 * 256)) * 256
N_IN = 3 * D_ATT + 2 * C_CONV + 2 * D_MODEL
RMS_EPS = 1e-6
LN_EPS = 1e-5
NEG_INF = -1e30
ATT_SCALE = HEAD_DIM ** -0.5

kernel_name = "dilated_window_conformer_conv_gated_hybrid_step"


def rms_norm(x, w):
    xf = x.astype(jnp.float32)
    y = xf * lax.rsqrt(jnp.mean(xf * xf, axis=-1, keepdims=True) + RMS_EPS)
    return (y * w.astype(jnp.float32)).astype(x.dtype)


def layer_norm(x, w, b):
    xf = x.astype(jnp.float32)
    mu = jnp.mean(xf, axis=-1, keepdims=True)
    xc = xf - mu
    y = xc * lax.rsqrt(jnp.mean(xc * xc, axis=-1, keepdims=True) + LN_EPS)
    return (y * w.astype(jnp.float32) + b.astype(jnp.float32)).astype(x.dtype)


def branch_inputs(x, norm_w, w_in, q_norm_w, k_norm_w):
    xn = rms_norm(x, norm_w)
    z = xn @ w_in
    lead = z.shape[:-1]
    o = 0
    q = z[..., o:o + D_ATT].reshape(*lead, N_GROUPS, HEADS_PER_GROUP, HEAD_DIM); o += D_ATT
    k = z[..., o:o + D_ATT].reshape(*lead, N_GROUPS, HEADS_PER_GROUP, HEAD_DIM); o += D_ATT
    v = z[..., o:o + D_ATT].reshape(*lead, N_GROUPS, HEADS_PER_GROUP, HEAD_DIM); o += D_ATT
    ua = z[..., o:o + C_CONV]; o += C_CONV
    ub = z[..., o:o + C_CONV]; o += C_CONV
    ga = z[..., o:o + D_MODEL]; o += D_MODEL
    gb = z[..., o:o + D_MODEL]
    q = rms_norm(q, q_norm_w)
    k = rms_norm(k, k_norm_w)
    u = ua * jax.nn.sigmoid(ub)
    return q, k, v, u, ga, gb


def window_attn_prompt(q, k, v, window, dil):
    B, T, H, dh = q.shape
    blk = window // dil
    seg = blk * dil
    Tp = -(-T // seg) * seg
    pad = Tp - T
    n = Tp // dil
    nb = n // blk

    def to_blocks(a):
        a = jnp.pad(a, ((0, 0), (0, pad), (0, 0), (0, 0)))
        a = a.reshape(B, n, dil, H, dh).transpose(0, 2, 1, 3, 4)
        return a.reshape(B, dil, nb, blk, H, dh)

    def with_prev(a):
        prev = jnp.pad(a[:, :, :-1], ((0, 0), (0, 0), (1, 0), (0, 0), (0, 0), (0, 0)))
        return jnp.concatenate([prev, a], axis=3)

    qb = to_blocks(q).astype(jnp.float32)
    kk = with_prev(to_blocks(k)).astype(jnp.float32)
    vv = with_prev(to_blocks(v)).astype(jnp.float32)
    s = jnp.einsum('brnqhd,brnkhd->brnhqk', qb, kk) * ATT_SCALE
    qi = jnp.arange(blk)[:, None]
    ki = jnp.arange(2 * blk)[None, :]
    dist = qi + blk - ki
    band = (dist >= 0) & (dist <= blk)
    has_prev = jnp.arange(nb)[:, None, None] > 0
    valid = band[None] & (has_prev | (ki >= blk)[None])
    s = jnp.where(valid[None, None, :, None], s, NEG_INF)
    lse = jax.nn.logsumexp(s, axis=-1)
    p = jnp.exp(s - lse[..., None])
    o = jnp.einsum('brnhqk,brnkhd->brnqhd', p, vv)
    o = o.reshape(B, dil, n, H, dh).transpose(0, 2, 1, 3, 4).reshape(B, Tp, H, dh)[:, :T]
    lse = lse.transpose(0, 1, 2, 4, 3).reshape(B, dil, n, H).transpose(0, 2, 1, 3).reshape(B, Tp, H)[:, :T]
    return o, lse


def window_attn_sample(q, ext_k, ext_v, window, dil):
    S = q.shape[1]
    L = ext_k.shape[1] - S
    J = window // dil
    idx = L + jnp.arange(S)[:, None] - dil * jnp.arange(J + 1)[None, :]
    valid = idx >= 0
    idx = jnp.maximum(idx, 0)
    kg = ext_k[:, idx].astype(jnp.float32)
    vg = ext_v[:, idx].astype(jnp.float32)
    s = jnp.einsum('bshd,bsjhd->bshj', q.astype(jnp.float32), kg) * ATT_SCALE
    s = jnp.where(valid[None, :, None, :], s, NEG_INF)
    lse = jax.nn.logsumexp(s, axis=-1)
    p = jnp.exp(s - lse[..., None])
    o = jnp.einsum('bshj,bsjhd->bshd', p, vg)
    return o, lse


def combine_groups(outs, lses):
    o = jnp.stack(outs, axis=0)
    alpha = jax.nn.softmax(jnp.stack(lses, axis=0), axis=0)
    return jnp.sum(alpha[..., None] * o, axis=0)


def causal_dwconv(u_ext, w, b):
    y = lax.conv_general_dilated(u_ext, w[:, None, :], window_strides=(1,), padding='VALID',
                                 dimension_numbers=('NWC', 'WIO', 'NWC'), feature_group_count=C_CONV)
    return y + b


def merge_and_ffn(x, o_att, c, ga, gb, w_att, conv_ln_w, conv_ln_b, w_conv_out, w_out,
                  norm_ffn_w, w_gate_up, w_down):
    a = o_att.reshape(*o_att.shape[:-2], D_ATT_OUT) @ w_att
    cb = jax.nn.silu(layer_norm(c, conv_ln_w, conv_ln_b)) @ w_conv_out
    h = jax.nn.sigmoid(ga) * a + jax.nn.sigmoid(gb) * cb
    x = x + h @ w_out
    gu = rms_norm(x, norm_ffn_w) @ w_gate_up
    g, up = gu[..., :D_FF], gu[..., D_FF:]
    return x + (jax.nn.silu(g) * up) @ w_down


def setup_inputs(seed: int = 0) -> dict:
    key = jax.random.key(seed)
    ks = jax.random.split(key, 24)
    f32 = jnp.float32
    nrm = lambda k, shape, scale: jax.random.normal(k, shape, f32) * scale
    lens = [min(w, PAST_LEN) for w in WINDOWS]
    return {
        "x_prompt": nrm(ks[0], (BATCH, SEQ, D_MODEL), 1.0),
        "x_sample": nrm(ks[1], (DEC_BATCH, DEC_SEQ, D_MODEL), 1.0),
        "cache_kv_w128": nrm(ks[2], (DEPTH, DEC_BATCH, lens[0], 2, HEADS_PER_GROUP, HEAD_DIM), 1.0),
        "cache_kv_w512": nrm(ks[3], (DEPTH, DEC_BATCH, lens[1], 2, HEADS_PER_GROUP, HEAD_DIM), 1.0),
        "cache_kv_w2048": nrm(ks[4], (DEPTH, DEC_BATCH, lens[2], 2, HEADS_PER_GROUP, HEAD_DIM), 1.0),
        "state_conv": nrm(ks[5], (DEPTH, DEC_BATCH, CONV_WIDTH - 1, C_CONV), 0.5),
        "norm_mix_w": 1.0 + nrm(ks[6], (DEPTH, D_MODEL), 0.02),
        "w_in": nrm(ks[7], (DEPTH, D_MODEL, N_IN), D_MODEL ** -0.5),
        "q_norm_w": 1.0 + nrm(ks[8], (DEPTH, HEAD_DIM), 0.02),
        "k_norm_w": 1.0 + nrm(ks[9], (DEPTH, HEAD_DIM), 0.02),
        "w_att": nrm(ks[10], (DEPTH, D_ATT_OUT, D_MODEL), D_ATT_OUT ** -0.5),
        "conv_w": nrm(ks[11], (DEPTH, CONV_WIDTH, C_CONV), CONV_WIDTH ** -0.5),
        "conv_b": nrm(ks[12], (DEPTH, C_CONV), 0.02),
        "conv_ln_w": 1.0 + nrm(ks[13], (DEPTH, C_CONV), 0.02),
        "conv_ln_b": nrm(ks[14], (DEPTH, C_CONV), 0.02),
        "w_conv_out": nrm(ks[15], (DEPTH, C_CONV, D_MODEL), C_CONV ** -0.5),
        "w_out": nrm(ks[16], (DEPTH, D_MODEL, D_MODEL), D_MODEL ** -0.5),
        "norm_ffn_w": 1.0 + nrm(ks[17], (DEPTH, D_MODEL), 0.02),
        "w_gate_up": nrm(ks[18], (DEPTH, D_MODEL, 2 * D_FF), D_MODEL ** -0.5),
        "w_down": nrm(ks[19], (DEPTH, D_FF, D_MODEL), D_FF ** -0.5),
    }


def reference(x_prompt, x_sample, cache_kv_w128, cache_kv_w512, cache_kv_w2048, state_conv,
              norm_mix_w, w_in, q_norm_w, k_norm_w, w_att, conv_w, conv_b, conv_ln_w, conv_ln_b,
              w_conv_out, w_out, norm_ffn_w, w_gate_up, w_down):
    caches = (cache_kv_w128, cache_kv_w512, cache_kv_w2048)
    xp, xs = x_prompt, x_sample
    p_kv = [[] for _ in WINDOWS]
    s_kv = [[] for _ in WINDOWS]
    p_conv, s_conv = [], []
    for l in range(DEPTH):
        q, k, v, u, ga, gb = branch_inputs(xp, norm_mix_w[l], w_in[l], q_norm_w[l], k_norm_w[l])
        T = xp.shape[1]
        outs, lses = [], []
        for g, (win, dil) in enumerate(zip(WINDOWS, DILATIONS)):
            o, lse = window_attn_prompt(q[:, :, g], k[:, :, g], v[:, :, g], win, dil)
            outs.append(o)
            lses.append(lse)
            lp = min(win, T)
            p_kv[g].append(jnp.stack([k[:, T - lp:, g], v[:, T - lp:, g]], axis=2))
        o_att = combine_groups(outs, lses).astype(xp.dtype)
        u_ext = jnp.pad(u, ((0, 0), (CONV_WIDTH - 1, 0), (0, 0)))
        c = causal_dwconv(u_ext, conv_w[l], conv_b[l])
        p_conv.append(u_ext[:, -(CONV_WIDTH - 1):])
        xp = merge_and_ffn(xp, o_att, c, ga, gb, w_att[l], conv_ln_w[l], conv_ln_b[l], w_conv_out[l],
                           w_out[l], norm_ffn_w[l], w_gate_up[l], w_down[l])

        q, k, v, u, ga, gb = branch_inputs(xs, norm_mix_w[l], w_in[l], q_norm_w[l], k_norm_w[l])
        outs, lses = [], []
        for g, (win, dil) in enumerate(zip(WINDOWS, DILATIONS)):
            new_kv = jnp.stack([k[:, :, g], v[:, :, g]], axis=2)
            ext = jnp.concatenate([caches[g][l].astype(new_kv.dtype), new_kv], axis=1)
            o, lse = window_attn_sample(q[:, :, g], ext[:, :, 0], ext[:, :, 1], win, dil)
            outs.append(o)
            lses.append(lse)
            ln = min(win, ext.shape[1])
            s_kv[g].append(ext[:, ext.shape[1] - ln:])
        o_att = combine_groups(outs, lses).astype(xs.dtype)
        u_ext = jnp.concatenate([state_conv[l].astype(u.dtype), u], axis=1)
        c = causal_dwconv(u_ext, conv_w[l], conv_b[l])
        s_conv.append(u_ext[:, -(CONV_WIDTH - 1):])
        xs = merge_and_ffn(xs, o_att, c, ga, gb, w_att[l], conv_ln_w[l], conv_ln_b[l], w_conv_out[l],
                           w_out[l], norm_ffn_w[l], w_gate_up[l], w_down[l])
    return (xp, xs,
            jnp.stack(p_kv[0]), jnp.stack(s_kv[0]),
            jnp.stack(p_kv[1]), jnp.stack(s_kv[1]),
            jnp.stack(p_kv[2]), jnp.stack(s_kv[2]),
            jnp.stack(p_conv), jnp.stack(s_conv))
```

```python
import functools

import jax
import jax.numpy as jnp
from jax import lax
from jax.experimental import pallas as pl
from jax.experimental.pallas import tpu as pltpu

F32 = jnp.float32
BF16 = jnp.bfloat16

D_MODEL = 1024
HEAD_DIM = 64
HEADS_PER_GROUP = 4
GROUP_W = HEADS_PER_GROUP * HEAD_DIM
N_GROUPS = 3
D_ATT = N_GROUPS * GROUP_W
WINDOWS = (128, 512, 2048)
DILATIONS = (1, 4, 16)
BAND = 128
SPAN = 2048
C_CONV = D_MODEL
CONV_WIDTH = 31
HALO = 32
D_FF = 2816
FF_CHUNK = 1408
N_IN = 3 * D_ATT + 2 * C_CONV + 2 * D_MODEL
RMS_EPS = 1e-6
LN_EPS = 1e-5
NEG_INF = -1e30
ATT_SCALE = HEAD_DIM ** -0.5
LANES = 128
VMEM_LIMIT = 56 * 1024 * 1024

O_Q, O_K, O_V = 0, D_ATT, 2 * D_ATT
O_UA = 3 * D_ATT
O_UB = O_UA + C_CONV
O_GA = O_UB + C_CONV
O_GB = O_GA + D_MODEL


def _dot(a, b):
    return jnp.dot(a, b, preferred_element_type=F32)


def _sigmoid(x):
    return 1.0 / (1.0 + jnp.exp(-x))


def _rms_rows(x, w):
    ms = jnp.mean(x * x, axis=-1, keepdims=True)
    return x * lax.rsqrt(ms + RMS_EPS) * w


def _head_norm_tiles(z, w_ref, scale):
    rows = z.shape[0]
    lane = lax.broadcasted_iota(jnp.int32, (rows, LANES), 1)
    low = lane < HEAD_DIM
    tiles = []
    for c in range(D_ATT // LANES):
        x = z[:, c * LANES:(c + 1) * LANES]
        x2 = x * x
        s_lo = jnp.sum(jnp.where(low, x2, 0.0), axis=-1, keepdims=True)
        s_hi = jnp.sum(jnp.where(low, 0.0, x2), axis=-1, keepdims=True)
        ms = jnp.where(low, s_lo, s_hi) * (1.0 / HEAD_DIM)
        y = x * lax.rsqrt(ms + RMS_EPS) * w_ref[:, c * LANES:(c + 1) * LANES]
        tiles.append(y * scale if scale != 1.0 else y)
    return tiles


def _in_projection(x, nw_ref, w_ref, qw_ref, kw_ref, store_qkv):
    xn = _rms_rows(x, nw_ref[...]).astype(BF16)
    for c, t in enumerate(_head_norm_tiles(_dot(xn, w_ref[:, O_Q:O_Q + D_ATT]), qw_ref, ATT_SCALE)):
        store_qkv(0, c, t)
    for c, t in enumerate(_head_norm_tiles(_dot(xn, w_ref[:, O_K:O_K + D_ATT]), kw_ref, 1.0)):
        store_qkv(1, c, t)
    v = _dot(xn, w_ref[:, O_V:O_V + D_ATT])
    for c in range(D_ATT // LANES):
        store_qkv(2, c, v[:, c * LANES:(c + 1) * LANES])
    ua = _dot(xn, w_ref[:, O_UA:O_UA + C_CONV])
    ub = _dot(xn, w_ref[:, O_UB:O_UB + C_CONV])
    u = ua * _sigmoid(ub)
    ga = _dot(xn, w_ref[:, O_GA:O_GA + D_MODEL])
    gb = _dot(xn, w_ref[:, O_GB:O_GB + D_MODEL])
    return u, ga, gb


def _inproj_prompt_kernel(x_ref, nw_ref, w_ref, qw_ref, kw_ref,
                          q0, q1, q2, k0, k1, k2, v0, v1, v2, u_ref, ga_ref, gb_ref,
                          t0, t1, t2, ct_ref, qs_ref, ks_ref, vs_ref, *, tm, n_t):
    i = pl.program_id(1)
    stage = (qs_ref, ks_ref, vs_ref)

    def store_qkv(which, c, tile):
        stage[which][c] = tile

    u, ga, gb = _in_projection(x_ref[0], nw_ref, w_ref, qw_ref, kw_ref, store_qkv)
    u_ref[0] = u
    ga_ref[0] = ga
    gb_ref[0] = gb
    tiles_per_group = GROUP_W // LANES
    for src, outs in ((qs_ref, (q0, q1, q2)), (ks_ref, (k0, k1, k2)), (vs_ref, (v0, v1, v2))):
        for g, dil in enumerate(DILATIONS):
            for p in range(tiles_per_group):
                c = g * tiles_per_group + p
                lanes = slice(p * LANES, (p + 1) * LANES)
                if dil == 1:
                    outs[g][0, 0, :, lanes] = src[c].astype(BF16)
                else:
                    for r in range(dil):
                        outs[g][0, r, :, lanes] = src[c, pl.ds(r, tm // dil, stride=dil), :].astype(BF16)
    for g, (win, t_ref) in enumerate(zip(WINDOWS, (t0, t1, t2))):
        rows = min(win, tm)
        first = n_t - win // rows

        @pl.when(i >= first)
        def _(t_ref=t_ref, g=g, rows=rows):
            for p in range(tiles_per_group):
                c = g * tiles_per_group + p
                t_ref[0, :, p * LANES:(p + 1) * LANES] = ks_ref[c, tm - rows:tm, :]
                t_ref[0, :, GROUP_W + p * LANES:GROUP_W + (p + 1) * LANES] = vs_ref[c, tm - rows:tm, :]

    @pl.when(i == n_t - 1)
    def _():
        ct_ref[0] = u[tm - HALO:tm, :]


def _inproj_sample_kernel(x_ref, nw_ref, w_ref, qw_ref, kw_ref, q_ref, k_ref, v_ref, u_ref, ga_ref, gb_ref):
    outs = (q_ref, k_ref, v_ref)

    def store_qkv(which, c, tile):
        outs[which][:, c * LANES:(c + 1) * LANES] = tile

    u, ga, gb = _in_projection(x_ref[...], nw_ref, w_ref, qw_ref, kw_ref, store_qkv)
    u_ref[...] = u
    ga_ref[...] = ga
    gb_ref[...] = gb


def _softmax_pv(s, vv):
    m = jnp.max(s, axis=-1, keepdims=True)
    p = jnp.exp(s - m)
    l = jnp.sum(p, axis=-1, keepdims=True)
    acc = _dot(p.astype(BF16), vv)
    return acc / l, m + jnp.log(l)


def _attn_unit(q2, kk, vv, lo_limit):
    qi = lax.broadcasted_iota(jnp.int32, (BAND, 2 * BAND), 0)
    ki = lax.broadcasted_iota(jnp.int32, (BAND, 2 * BAND), 1)
    valid = (ki >= qi) & (ki <= qi + BAND) & (ki >= lo_limit)
    lane = lax.broadcasted_iota(jnp.int32, (BAND, LANES), 1)
    low = lane < HEAD_DIM
    res = []
    for keep in (low, jnp.logical_not(low)):
        qm = jnp.where(keep, q2, jnp.zeros_like(q2))
        s = lax.dot_general(qm, kk, (((1,), (1,)), ((), ())), preferred_element_type=F32)
        s = jnp.where(valid, s, NEG_INF)
        res.append(_softmax_pv(s, vv))
    o2 = jnp.where(low, res[0][0], res[1][0])
    l2 = jnp.where(low, res[0][1], res[1][1])
    return o2, l2


def _attn_prompt_kernel(q0, k0, v0, kp0, vp0, q1, k1, v1, kp1, vp1, q2, k2, v2, kp2, vp2,
                        o_ref, og_ref, lg_ref):
    sb = pl.program_id(1)
    refs = ((q0, k0, v0, kp0, vp0), (q1, k1, v1, kp1, vp1), (q2, k2, v2, kp2, vp2))
    first_lo = jnp.where(sb > 0, 0, BAND)
    tiles = GROUP_W // LANES

    for g in range(N_GROUPS):
        dil, win = DILATIONS[g], WINDOWS[g]
        q_ref, k_ref, v_ref, kp_ref, vp_ref = refs[g]

        def unit(seg, r, first, g=g, dil=dil, win=win, q_ref=q_ref, k_ref=k_ref, v_ref=v_ref,
                 kp_ref=kp_ref, vp_ref=vp_ref):
            row0 = seg * BAND
            if not isinstance(row0, int):
                row0 = pl.multiple_of(row0, BAND)
            for p in range(tiles):
                lanes = slice(p * LANES, (p + 1) * LANES)
                qq = q_ref[0, r, pl.ds(row0, BAND), lanes]
                if first:
                    kk = jnp.concatenate([kp_ref[0, r, :, lanes], k_ref[0, r, 0:BAND, lanes]], axis=0)
                    vv = jnp.concatenate([vp_ref[0, r, :, lanes], v_ref[0, r, 0:BAND, lanes]], axis=0)
                    lo = first_lo
                else:
                    kk = k_ref[0, r, pl.ds(row0 - BAND, 2 * BAND), lanes]
                    vv = v_ref[0, r, pl.ds(row0 - BAND, 2 * BAND), lanes]
                    lo = 0
                o2, l2 = _attn_unit(qq, kk, vv, lo)
                start = seg * win + r
                rows = pl.ds(start, BAND) if dil == 1 else pl.ds(start, BAND, stride=dil)
                og_ref[g * tiles + p, rows, :] = o2
                lg_ref[g * tiles + p, rows, :] = l2

        if dil == 1:
            unit(0, 0, True)
        else:
            def first_body(r, c, unit=unit):
                unit(0, r, True)
                return c
            lax.fori_loop(0, dil, first_body, 0)
        n_rest = (SPAN // win - 1) * dil
        if n_rest:
            def rest_body(t, c, unit=unit, dil=dil):
                unit(1 + (t >> (dil.bit_length() - 1)), t & (dil - 1), False)
                return c
            lax.fori_loop(0, n_rest, rest_body, 0)

    chunk = 256

    def merge_body(c, carry):
        rows = pl.ds(pl.multiple_of(c * chunk, chunk), chunk)
        for p in range(tiles):
            l0, l1, l2 = lg_ref[p, rows, :], lg_ref[tiles + p, rows, :], lg_ref[2 * tiles + p, rows, :]
            m = jnp.maximum(jnp.maximum(l0, l1), l2)
            w0, w1, w2 = jnp.exp(l0 - m), jnp.exp(l1 - m), jnp.exp(l2 - m)
            num = (w0 * og_ref[p, rows, :] + w1 * og_ref[tiles + p, rows, :]
                   + w2 * og_ref[2 * tiles + p, rows, :])
            o_ref[0, rows, p * LANES:(p + 1) * LANES] = (num / (w0 + w1 + w2)).astype(o_ref.dtype)
        return carry

    lax.fori_loop(0, SPAN // chunk, merge_body, 0)


def _merge_math(x, o_att, c, ga, gb, watt_ref, lnw_ref, lnb_ref, wco_ref, wout_ref):
    a = _dot(o_att.astype(BF16), watt_ref[...])
    mu = jnp.mean(c, axis=-1, keepdims=True)
    xc = c - mu
    var = jnp.mean(xc * xc, axis=-1, keepdims=True)
    y = xc * lax.rsqrt(var + LN_EPS) * lnw_ref[...] + lnb_ref[...]
    act = y * _sigmoid(y)
    cb = _dot(act.astype(BF16), wco_ref[...])
    h = _sigmoid(ga) * a + _sigmoid(gb) * cb
    return x + _dot(h.astype(BF16), wout_ref[...])


def _merge_prompt_kernel(x_ref, oa_ref, u_ref, uh_ref, ga_ref, gb_ref, watt_ref, cw_ref, cb_ref,
                         lnw_ref, lnb_ref, wco_ref, wout_ref, y_ref, ubuf_ref, cbuf_ref, *, tm):
    i = pl.program_id(1)
    ubuf_ref[0:HALO, :] = jnp.where(i > 0, uh_ref[0], 0.0)
    ubuf_ref[HALO:HALO + tm, :] = u_ref[0]
    rc, lc = 32, 256
    off = HALO - (CONV_WIDTH - 1)
    for r0 in range(0, tm, rc):
        for l0 in range(0, C_CONV, lc):
            lanes = slice(l0, l0 + lc)
            acc = jnp.broadcast_to(cb_ref[:, lanes], (rc, lc))
            for j in range(CONV_WIDTH):
                acc = acc + ubuf_ref[r0 + off + j:r0 + off + j + rc, lanes] * cw_ref[j:j + 1, lanes]
            cbuf_ref[r0:r0 + rc, lanes] = acc
    y_ref[0] = _merge_math(x_ref[0], oa_ref[0], cbuf_ref[...], ga_ref[0], gb_ref[0],
                           watt_ref, lnw_ref, lnb_ref, wco_ref, wout_ref)


def _merge_sample_kernel(x_ref, oa_ref, c_ref, ga_ref, gb_ref, watt_ref, lnw_ref, lnb_ref, wco_ref, wout_ref,
                         y_ref):
    y_ref[...] = _merge_math(x_ref[...], oa_ref[...], c_ref[...], ga_ref[...], gb_ref[...],
                             watt_ref, lnw_ref, lnb_ref, wco_ref, wout_ref)


def _ffn_kernel(x_ref, nw_ref, wgu_ref, wd_ref, y_ref):
    x = x_ref[...]
    xn = _rms_rows(x, nw_ref[...]).astype(BF16)
    acc = x
    for c0 in range(0, D_FF, FF_CHUNK):
        g = _dot(xn, wgu_ref[:, c0:c0 + FF_CHUNK])
        up = _dot(xn, wgu_ref[:, D_FF + c0:D_FF + c0 + FF_CHUNK])
        act = (g * _sigmoid(g) * up).astype(BF16)
        acc = acc + _dot(act, wd_ref[c0:c0 + FF_CHUNK, :])
    y_ref[...] = acc


def _sample_step_kernel(q_ref, kn_ref, vn_ref, c0_ref, c1_ref, c2_ref, st_ref, u_ref, cw_ref, cb_ref,
                        o_ref, s0_ref, s1_ref, s2_ref, c_ref, cs_ref, ue_ref, new8_ref, *, n_new):
    caches = (c0_ref, c1_ref, c2_ref)
    outs = (s0_ref, s1_ref, s2_ref)
    pad = 8
    row = lax.broadcasted_iota(jnp.int32, (HEADS_PER_GROUP * pad, GROUP_W), 0)
    col = lax.broadcasted_iota(jnp.int32, (HEADS_PER_GROUP * pad, GROUP_W), 1)
    head_lanes = (col >> 6) == (row >> 3)
    new8_ref[...] = jnp.zeros(new8_ref.shape, F32)
    new8_ref[0, 0:n_new, :] = q_ref[0]
    new8_ref[1, 0:n_new, :] = kn_ref[0]
    new8_ref[2, 0:n_new, :] = vn_ref[0]
    o_parts, l_parts = [], []
    for g, (win, dil) in enumerate(zip(WINDOWS, DILATIONS)):
        cols = slice(g * GROUP_W, (g + 1) * GROUP_W)
        cache = caches[g]
        outs[g][0, 0:win - n_new, :] = cache[0, n_new:win, :]
        outs[g][0, win - n_new:win, 0:GROUP_W] = kn_ref[0, :, cols]
        outs[g][0, win - n_new:win, GROUP_W:2 * GROUP_W] = vn_ref[0, :, cols]
        q8 = new8_ref[0, :, cols]
        qm = jnp.where(head_lanes, jnp.concatenate([q8] * HEADS_PER_GROUP, axis=0), 0.0).astype(BF16)
        kc = cache[0, :, 0:GROUP_W].astype(BF16)
        vc = cache[0, :, GROUP_W:2 * GROUP_W].astype(BF16)
        kn8 = new8_ref[1, :, cols].astype(BF16)
        vn8 = new8_ref[2, :, cols].astype(BF16)
        nt = (((1,), (1,)), ((), ()))
        s_c = lax.dot_general(qm, kc, nt, preferred_element_type=F32)
        s_n = lax.dot_general(qm, kn8, nt, preferred_element_type=F32)
        sq_c = lax.broadcasted_iota(jnp.int32, s_c.shape, 0) & (pad - 1)
        kc_i = lax.broadcasted_iota(jnp.int32, s_c.shape, 1)
        d_c = win + sq_c - kc_i
        ok_c = ((d_c & (dil - 1)) == 0) & (d_c <= win)
        sq_n = lax.broadcasted_iota(jnp.int32, s_n.shape, 0) & (pad - 1)
        kn_i = lax.broadcasted_iota(jnp.int32, s_n.shape, 1)
        d_n = sq_n - kn_i
        ok_n = (d_n >= 0) & ((d_n & (dil - 1)) == 0) & (kn_i < n_new)
        s_c = jnp.where(ok_c, s_c, NEG_INF)
        s_n = jnp.where(ok_n, s_n, NEG_INF)
        m = jnp.maximum(jnp.max(s_c, axis=-1, keepdims=True), jnp.max(s_n, axis=-1, keepdims=True))
        p_c = jnp.exp(s_c - m)
        p_n = jnp.exp(s_n - m)
        l = jnp.sum(p_c, axis=-1, keepdims=True) + jnp.sum(p_n, axis=-1, keepdims=True)
        acc = _dot(p_c.astype(BF16), vc) + _dot(p_n.astype(BF16), vn8)
        o_parts.append(acc / l)
        l_parts.append(m + jnp.log(l))
    m = jnp.maximum(jnp.maximum(l_parts[0], l_parts[1]), l_parts[2])
    w = [jnp.exp(lp - m) for lp in l_parts]
    num = w[0] * o_parts[0] + w[1] * o_parts[1] + w[2] * o_parts[2]
    om = jnp.where(head_lanes, num / (w[0] + w[1] + w[2]), 0.0)
    o8 = om[0:pad] + om[pad:2 * pad] + om[2 * pad:3 * pad] + om[3 * pad:4 * pad]
    o_ref[0] = o8[0:n_new, :]

    n_state = CONV_WIDTH - 1
    ue_ref[0:n_state, :] = st_ref[0]
    ue_ref[n_state:n_state + n_new, :] = u_ref[0]
    ue_ref[n_state + n_new:, :] = jnp.zeros((ue_ref.shape[0] - n_state - n_new, C_CONV), F32)
    acc = jnp.broadcast_to(cb_ref[...], (pad, C_CONV))
    for j in range(CONV_WIDTH):
        acc = acc + ue_ref[j:j + pad, :] * cw_ref[j:j + 1, :]
    c_ref[0] = acc[0:n_new, :]
    cs_ref[0] = ue_ref[n_new:n_new + n_state, :]


def _const_spec(shape):
    nd = len(shape)
    return pl.BlockSpec(shape, lambda *_: (0,) * nd)


def _params(*sem):
    return pltpu.CompilerParams(dimension_semantics=sem, vmem_limit_bytes=VMEM_LIMIT)


def kernel(x_prompt, x_sample, cache_kv_w128, cache_kv_w512, cache_kv_w2048, state_conv, norm_mix_w, w_in,
           q_norm_w, k_norm_w, w_att, conv_w, conv_b, conv_ln_w, conv_ln_b, w_conv_out, w_out, norm_ffn_w,
           w_gate_up, w_down):
    B, T, D = x_prompt.shape
    SB, SS, _ = x_sample.shape
    assert D == D_MODEL and T % SPAN == 0 and norm_mix_w.shape[0] == 1
    caches = (cache_kv_w128, cache_kv_w512, cache_kv_w2048)
    for c, win in zip(caches, WINDOWS):
        assert c.shape[2] == win, "cached window shorter than the attention window is not supported"

    nmw = norm_mix_w.reshape(1, D)
    w_in_b = w_in[0].astype(BF16)
    qw = jnp.tile(q_norm_w[0], D_ATT // HEAD_DIM).reshape(1, D_ATT)
    kw = jnp.tile(k_norm_w[0], D_ATT // HEAD_DIM).reshape(1, D_ATT)
    w_att_b = w_att[0].astype(BF16)
    cw = conv_w[0]
    cb = conv_b.reshape(1, C_CONV)
    lnw = conv_ln_w.reshape(1, C_CONV)
    lnb = conv_ln_b.reshape(1, C_CONV)
    wco_b = w_conv_out[0].astype(BF16)
    wout_b = w_out[0].astype(BF16)
    nfw = norm_ffn_w.reshape(1, D)
    wgu_b = w_gate_up[0].astype(BF16)
    wd_b = w_down[0].astype(BF16)

    tm = 256
    n_t = T // tm
    perm_shapes = [jax.ShapeDtypeStruct((B, dil, T // dil, GROUP_W), BF16) for dil in DILATIONS]
    perm_specs = [pl.BlockSpec((1, dil, tm // dil, GROUP_W), lambda b, i: (b, 0, i, 0)) for dil in DILATIONS]
    tail_rows = [min(win, tm) for win in WINDOWS]
    tail_shapes = [jax.ShapeDtypeStruct((B, win, 2 * GROUP_W), F32) for win in WINDOWS]
    tail_specs = [
        pl.BlockSpec((1, rows, 2 * GROUP_W),
                     functools.partial(lambda b, i, first: (b, jnp.maximum(i - first, 0), 0),
                                       first=n_t - win // rows))
        for win, rows in zip(WINDOWS, tail_rows)]
    tok_spec = pl.BlockSpec((1, tm, D), lambda b, i: (b, i, 0))
    tok_shape = jax.ShapeDtypeStruct((B, T, D), F32)
    res = pl.pallas_call(
        functools.partial(_inproj_prompt_kernel, tm=tm, n_t=n_t),
        grid=(B, n_t),
        in_specs=[tok_spec, _const_spec((1, D)), _const_spec((D, N_IN)), _const_spec((1, D_ATT)),
                  _const_spec((1, D_ATT))],
        out_specs=perm_specs * 3 + [tok_spec] * 3 + tail_specs
        + [pl.BlockSpec((1, HALO, C_CONV), lambda b, i: (b, 0, 0))],
        out_shape=perm_shapes * 3 + [tok_shape] * 3 + tail_shapes
        + [jax.ShapeDtypeStruct((B, HALO, C_CONV), F32)],
        scratch_shapes=[pltpu.VMEM((D_ATT // LANES, tm, LANES), F32)] * 3,
        compiler_params=_params("arbitrary", "arbitrary"),
        name="inproj_prompt",
    )(x_prompt, nmw, w_in_b, qw, kw)
    qp, kp, vp = res[0:3], res[3:6], res[6:9]
    u_p, ga_p, gb_p = res[9:12]
    tails_p = res[12:15]
    conv_tail_p = res[15]

    n_sb = T // SPAN
    att_in, att_specs = [], []
    for g, dil in enumerate(DILATIONS):
        rows = SPAN // dil
        cur = pl.BlockSpec((1, dil, rows, GROUP_W), lambda b, s: (b, 0, s, 0))
        prev = pl.BlockSpec((1, dil, BAND, GROUP_W),
                            functools.partial(lambda b, s, n: (b, 0, jnp.maximum(s * n - 1, 0), 0),
                                              n=rows // BAND))
        att_in += [qp[g], kp[g], vp[g], kp[g], vp[g]]
        att_specs += [cur, cur, cur, prev, prev]
    o_att_p = pl.pallas_call(
        _attn_prompt_kernel,
        grid=(B, n_sb),
        in_specs=att_specs,
        out_specs=pl.BlockSpec((1, SPAN, GROUP_W), lambda b, s: (b, s, 0)),
        out_shape=jax.ShapeDtypeStruct((B, T, GROUP_W), BF16),
        scratch_shapes=[pltpu.VMEM((N_GROUPS * GROUP_W // LANES, SPAN, LANES), F32)] * 2,
        compiler_params=_params("arbitrary", "arbitrary"),
        name="attn_prompt",
    )(*att_in)

    tm3 = 256
    tok3 = pl.BlockSpec((1, tm3, D), lambda b, i: (b, i, 0))
    merge_w_specs = [_const_spec((GROUP_W, D)), _const_spec((CONV_WIDTH, C_CONV)), _const_spec((1, C_CONV)),
                     _const_spec((1, C_CONV)), _const_spec((1, C_CONV)), _const_spec((C_CONV, D)),
                     _const_spec((D, D))]
    x1_p = pl.pallas_call(
        functools.partial(_merge_prompt_kernel, tm=tm3),
        grid=(B, T // tm3),
        in_specs=[tok3, pl.BlockSpec((1, tm3, GROUP_W), lambda b, i: (b, i, 0)), tok3,
                  pl.BlockSpec((1, HALO, C_CONV), lambda b, i: (b, jnp.maximum(i * (tm3 // HALO) - 1, 0), 0)),
                  tok3, tok3] + merge_w_specs,
        out_specs=tok3,
        out_shape=tok_shape,
        scratch_shapes=[pltpu.VMEM((tm3 + HALO, C_CONV), F32), pltpu.VMEM((tm3, C_CONV), F32)],
        compiler_params=_params("arbitrary", "arbitrary"),
        name="merge_prompt",
    )(x_prompt, o_att_p, u_p, u_p, ga_p, gb_p, w_att_b, cw, cb, lnw, lnb, wco_b, wout_b)

    tmf = 256
    n_tok = B * T
    ffn_w_specs = [_const_spec((1, D)), _const_spec((D, 2 * D_FF)), _const_spec((D_FF, D))]
    y_p = pl.pallas_call(
        _ffn_kernel,
        grid=(n_tok // tmf,),
        in_specs=[pl.BlockSpec((tmf, D), lambda i: (i, 0))] + ffn_w_specs,
        out_specs=pl.BlockSpec((tmf, D), lambda i: (i, 0)),
        out_shape=jax.ShapeDtypeStruct((n_tok, D), F32),
        compiler_params=_params("arbitrary"),
        name="ffn_prompt",
    )(x1_p.reshape(n_tok, D), nfw, wgu_b, wd_b).reshape(B, T, D)

    n_s = SB * SS
    xs2 = x_sample.reshape(n_s, D)
    row_shapes = ([jax.ShapeDtypeStruct((n_s, D_ATT), F32)] * 3 + [jax.ShapeDtypeStruct((n_s, C_CONV), F32)]
                  + [jax.ShapeDtypeStruct((n_s, D), F32)] * 2)
    q_s, k_s, v_s, u_s, ga_s, gb_s = pl.pallas_call(
        _inproj_sample_kernel,
        grid=(1,),
        in_specs=[_const_spec((n_s, D)), _const_spec((1, D)), _const_spec((D, N_IN)), _const_spec((1, D_ATT)),
                  _const_spec((1, D_ATT))],
        out_specs=[_const_spec(s.shape) for s in row_shapes],
        out_shape=row_shapes,
        compiler_params=_params("arbitrary"),
        name="inproj_sample",
    )(xs2, nmw, w_in_b, qw, kw)

    seq_spec = lambda rows, width: pl.BlockSpec((1, rows, width), lambda b: (b, 0, 0))
    cache2 = [c[0].reshape(SB, win, 2 * GROUP_W) for c, win in zip(caches, WINDOWS)]
    n_state = CONV_WIDTH - 1
    step_out_shapes = ([jax.ShapeDtypeStruct((SB, SS, GROUP_W), F32)]
                       + [jax.ShapeDtypeStruct((SB, win, 2 * GROUP_W), F32) for win in WINDOWS]
                       + [jax.ShapeDtypeStruct((SB, SS, C_CONV), F32),
                          jax.ShapeDtypeStruct((SB, n_state, C_CONV), F32)])
    o_att_s, kv0_s, kv1_s, kv2_s, c_s, conv_state_s = pl.pallas_call(
        functools.partial(_sample_step_kernel, n_new=SS),
        grid=(SB,),
        in_specs=[seq_spec(SS, D_ATT)] * 3 + [seq_spec(win, 2 * GROUP_W) for win in WINDOWS]
        + [seq_spec(n_state, C_CONV), seq_spec(SS, C_CONV), _const_spec((CONV_WIDTH, C_CONV)),
           _const_spec((1, C_CONV))],
        out_specs=[seq_spec(SS, GROUP_W)] + [seq_spec(win, 2 * GROUP_W) for win in WINDOWS]
        + [seq_spec(SS, C_CONV), seq_spec(n_state, C_CONV)],
        out_shape=step_out_shapes,
        scratch_shapes=[pltpu.VMEM((CONV_WIDTH + 9, C_CONV), F32), pltpu.VMEM((3, 8, D_ATT), F32)],
        compiler_params=_params("arbitrary"),
        name="sample_step",
    )(q_s.reshape(SB, SS, D_ATT), k_s.reshape(SB, SS, D_ATT), v_s.reshape(SB, SS, D_ATT), *cache2,
      state_conv[0], u_s.reshape(SB, SS, C_CONV), cw, cb)

    x1_s = pl.pallas_call(
        _merge_sample_kernel,
        grid=(1,),
        in_specs=[_const_spec((n_s, D)), _const_spec((n_s, GROUP_W)), _const_spec((n_s, C_CONV)),
                  _const_spec((n_s, D)), _const_spec((n_s, D)), _const_spec((GROUP_W, D)),
                  _const_spec((1, C_CONV)), _const_spec((1, C_CONV)), _const_spec((C_CONV, D)),
                  _const_spec((D, D))],
        out_specs=_const_spec((n_s, D)),
        out_shape=jax.ShapeDtypeStruct((n_s, D), F32),
        compiler_params=_params("arbitrary"),
        name="merge_sample",
    )(xs2, o_att_s.reshape(n_s, GROUP_W), c_s.reshape(n_s, C_CONV), ga_s, gb_s, w_att_b, lnw, lnb, wco_b, wout_b)

    y_s = pl.pallas_call(
        _ffn_kernel,
        grid=(1,),
        in_specs=[_const_spec((n_s, D))] + ffn_w_specs,
        out_specs=_const_spec((n_s, D)),
        out_shape=jax.ShapeDtypeStruct((n_s, D), F32),
        compiler_params=_params("arbitrary"),
        name="ffn_sample",
    )(x1_s, nfw, wgu_b, wd_b).reshape(SB, SS, D)

    def kv_shape(a, nb, win):
        return a.reshape(1, nb, win, 2, HEADS_PER_GROUP, HEAD_DIM)

    return (y_p, y_s,
            kv_shape(tails_p[0], B, WINDOWS[0]), kv_shape(kv0_s, SB, WINDOWS[0]),
            kv_shape(tails_p[1], B, WINDOWS[1]), kv_shape(kv1_s, SB, WINDOWS[1]),
            kv_shape(tails_p[2], B, WINDOWS[2]), kv_shape(kv2_s, SB, WINDOWS[2]),
            conv_tail_p[:, HALO - n_state:, :][None], conv_state_s[None])
```

```python
import functools
import math

import jax
import jax.numpy as jnp
from jax import lax
from jax.experimental import pallas as pl
from jax.experimental.pallas import tpu as pltpu

F32 = jnp.float32
BF16 = jnp.bfloat16

D_MODEL = 1024
HEAD_DIM = 64
HEADS_PER_GROUP = 4
GROUP_W = HEADS_PER_GROUP * HEAD_DIM
N_GROUPS = 3
D_ATT = N_GROUPS * GROUP_W
WINDOWS = (128, 512, 2048)
DILATIONS = (1, 4, 16)
BAND = 128
SPAN = 2048
C_CONV = D_MODEL
CONV_WIDTH = 31
HALO = 32
D_FF = 2816
FF_CHUNK = 1408
N_IN = 3 * D_ATT + 2 * C_CONV + 2 * D_MODEL
RMS_EPS = 1e-6
LN_EPS = 1e-5
NEG_INF = -1e30
ATT_SCALE = HEAD_DIM ** -0.5
Q_SCALE = ATT_SCALE * math.log2(math.e)
LANES = 128
LANE_TILES = C_CONV // LANES
VMEM_LIMIT = 56 * 1024 * 1024

O_Q, O_K, O_V = 0, D_ATT, 2 * D_ATT
O_UA = 3 * D_ATT
O_UB = O_UA + C_CONV
O_GA = O_UB + C_CONV
O_GB = O_GA + D_MODEL


def _dot(a, b):
    return jnp.dot(a, b, preferred_element_type=F32)


def _sigmoid(x):
    return 1.0 / (1.0 + jnp.exp(-x))


def _rms_rows(x, w):
    ms = jnp.mean(x * x, axis=-1, keepdims=True)
    return x * lax.rsqrt(ms + RMS_EPS) * w


def _head_norm_tiles(z, w_ref, scale):
    rows = z.shape[0]
    lane = lax.broadcasted_iota(jnp.int32, (rows, LANES), 1)
    low = lane < HEAD_DIM
    tiles = []
    for c in range(D_ATT // LANES):
        x = z[:, c * LANES:(c + 1) * LANES]
        x2 = x * x
        s_lo = jnp.sum(jnp.where(low, x2, 0.0), axis=-1, keepdims=True)
        s_hi = jnp.sum(jnp.where(low, 0.0, x2), axis=-1, keepdims=True)
        ms = jnp.where(low, s_lo, s_hi) * (1.0 / HEAD_DIM)
        y = x * lax.rsqrt(ms + RMS_EPS) * w_ref[:, c * LANES:(c + 1) * LANES]
        tiles.append(y * scale if scale != 1.0 else y)
    return tiles


def _in_projection(x, nw_ref, w_ref, qw_ref, kw_ref, store_qkv):
    xn = _rms_rows(x, nw_ref[...]).astype(BF16)
    for c, t in enumerate(_head_norm_tiles(_dot(xn, w_ref[:, O_Q:O_Q + D_ATT]), qw_ref, Q_SCALE)):
        store_qkv(0, c, t)
    for c, t in enumerate(_head_norm_tiles(_dot(xn, w_ref[:, O_K:O_K + D_ATT]), kw_ref, 1.0)):
        store_qkv(1, c, t)
    v = _dot(xn, w_ref[:, O_V:O_V + D_ATT])
    for c in range(D_ATT // LANES):
        store_qkv(2, c, v[:, c * LANES:(c + 1) * LANES])
    ua = _dot(xn, w_ref[:, O_UA:O_UA + C_CONV])
    ub = _dot(xn, w_ref[:, O_UB:O_UB + C_CONV])
    u = ua * _sigmoid(ub)
    sa = _sigmoid(_dot(xn, w_ref[:, O_GA:O_GA + D_MODEL]))
    sb = _sigmoid(_dot(xn, w_ref[:, O_GB:O_GB + D_MODEL]))
    return u, sa, sb


def _inproj_prompt_kernel(x_ref, nw_ref, w_ref, qw_ref, kw_ref,
                          q0, q1, q2, k0, k1, k2, v0, v1, v2, u_ref, sa_ref, sb_ref,
                          t0, t1, t2, ct_ref, qs_ref, ks_ref, vs_ref, *, tm, n_t):
    i = pl.program_id(1)
    stage = (qs_ref, ks_ref, vs_ref)

    def store_qkv(which, c, tile):
        stage[which][c] = tile

    u, sa, sb = _in_projection(x_ref[0], nw_ref, w_ref, qw_ref, kw_ref, store_qkv)
    u_ref[0] = u
    sa_ref[0] = sa.astype(sa_ref.dtype)
    sb_ref[0] = sb.astype(sb_ref.dtype)
    tiles_per_group = GROUP_W // LANES
    for src, outs in ((qs_ref, (q0, q1, q2)), (ks_ref, (k0, k1, k2)), (vs_ref, (v0, v1, v2))):
        for g, dil in enumerate(DILATIONS):
            for p in range(tiles_per_group):
                c = g * tiles_per_group + p
                lanes = slice(p * LANES, (p + 1) * LANES)
                if dil == 1:
                    outs[g][0, 0, :, lanes] = src[c].astype(BF16)
                else:
                    for r in range(dil):
                        outs[g][0, r, :, lanes] = src[c, pl.ds(r, tm // dil, stride=dil), :].astype(BF16)
    for g, (win, t_ref) in enumerate(zip(WINDOWS, (t0, t1, t2))):
        rows = min(win, tm)
        first = n_t - win // rows

        @pl.when(i >= first)
        def _(t_ref=t_ref, g=g, rows=rows):
            for p in range(tiles_per_group):
                c = g * tiles_per_group + p
                t_ref[0, :, p * LANES:(p + 1) * LANES] = ks_ref[c, tm - rows:tm, :]
                t_ref[0, :, GROUP_W + p * LANES:GROUP_W + (p + 1) * LANES] = vs_ref[c, tm - rows:tm, :]

    @pl.when(i == n_t - 1)
    def _():
        ct_ref[0] = u[tm - HALO:tm, :]


def _inproj_sample_kernel(x_ref, nw_ref, w_ref, qw_ref, kw_ref, q_ref, k_ref, v_ref, u_ref, sa_ref, sb_ref):
    outs = (q_ref, k_ref, v_ref)

    def store_qkv(which, c, tile):
        outs[which][:, c * LANES:(c + 1) * LANES] = tile

    u, sa, sb = _in_projection(x_ref[...], nw_ref, w_ref, qw_ref, kw_ref, store_qkv)
    u_ref[...] = u
    sa_ref[...] = sa.astype(sa_ref.dtype)
    sb_ref[...] = sb.astype(sb_ref.dtype)


def _attn_unit(qs, kks, vvs, lo_limits):
    qi = lax.broadcasted_iota(jnp.int32, (2 * BAND, 2 * BAND), 0) & (BAND - 1)
    ki = lax.broadcasted_iota(jnp.int32, (2 * BAND, 2 * BAND), 1)
    band = (ki >= qi) & (ki <= qi + BAND)
    low = lax.broadcasted_iota(jnp.int32, (BAND, LANES), 1) < HEAD_DIM
    scores = []
    for q2, kk in zip(qs, kks):
        zero = jnp.zeros_like(q2)
        q_st = jnp.concatenate([jnp.where(low, q2, zero), jnp.where(low, zero, q2)], axis=0)
        scores.append(lax.dot_general(q_st, kk, (((1,), (1,)), ((), ())), preferred_element_type=F32))
    res = []
    for s, vv, lo in zip(scores, vvs, lo_limits):
        valid = band if isinstance(lo, int) and lo == 0 else band & (ki >= lo)
        s = jnp.where(valid, s, NEG_INF)
        m = jnp.max(s, axis=-1, keepdims=True)
        p = jnp.exp2(s - m)
        l = jnp.sum(p, axis=-1, keepdims=True)
        o = _dot(p.astype(BF16), vv) / l
        lse = m + jnp.log2(l)
        res.append((jnp.where(low, o[0:BAND], o[BAND:2 * BAND]),
                    jnp.where(low, lse[0:BAND], lse[BAND:2 * BAND])))
    return res


def _attn_prompt_kernel(q0, k0, v0, kp0, vp0, q1, k1, v1, kp1, vp1, q2, k2, v2, kp2, vp2,
                        o_ref, og_ref, lg_ref):
    sb = pl.program_id(1)
    refs = ((q0, k0, v0, kp0, vp0), (q1, k1, v1, kp1, vp1), (q2, k2, v2, kp2, vp2))
    first_lo = jnp.where(sb > 0, 0, BAND)
    tiles = GROUP_W // LANES

    for g in range(N_GROUPS):
        dil, win = DILATIONS[g], WINDOWS[g]
        q_ref, k_ref, v_ref, kp_ref, vp_ref = refs[g]

        def units(blocks, g=g, dil=dil, win=win, q_ref=q_ref, k_ref=k_ref, v_ref=v_ref,
                  kp_ref=kp_ref, vp_ref=vp_ref):
            qs, kks, vvs, los, dst = [], [], [], [], []
            for seg, r, first in blocks:
                row0 = seg * BAND
                if not isinstance(row0, int):
                    row0 = pl.multiple_of(row0, BAND)
                start = seg * win + r
                rows = pl.ds(start, BAND) if dil == 1 else pl.ds(start, BAND, stride=dil)
                for p in range(tiles):
                    lanes = slice(p * LANES, (p + 1) * LANES)
                    qs.append(q_ref[0, r, pl.ds(row0, BAND), lanes])
                    if first:
                        kks.append(jnp.concatenate([kp_ref[0, r, :, lanes], k_ref[0, r, 0:BAND, lanes]], axis=0))
                        vvs.append(jnp.concatenate([vp_ref[0, r, :, lanes], v_ref[0, r, 0:BAND, lanes]], axis=0))
                        los.append(first_lo)
                    else:
                        kks.append(k_ref[0, r, pl.ds(row0 - BAND, 2 * BAND), lanes])
                        vvs.append(v_ref[0, r, pl.ds(row0 - BAND, 2 * BAND), lanes])
                        los.append(0)
                    dst.append((g * tiles + p, rows))
            for (slot, rows), (o2, l2) in zip(dst, _attn_unit(qs, kks, vvs, los)):
                og_ref[slot, rows, :] = o2
                lg_ref[slot, rows, :] = l2

        shift = dil.bit_length() - 1
        if dil == 1:
            units([(0, 0, True), (1, 0, False)])

            def rest_body(t, c, units=units):
                units([(2 + 2 * t, 0, False), (3 + 2 * t, 0, False)])
                return c
            lax.fori_loop(0, (SPAN // win - 2) // 2, rest_body, 0)
        else:
            def first_body(t, c, units=units):
                units([(0, 2 * t, True), (0, 2 * t + 1, True)])
                return c
            lax.fori_loop(0, dil // 2, first_body, 0)
            n_rest = (SPAN // win - 1) * dil
            if n_rest:
                def rest_body(t, c, units=units, dil=dil, shift=shift):
                    a, b = 2 * t, 2 * t + 1
                    units([(1 + (a >> shift), a & (dil - 1), False), (1 + (b >> shift), b & (dil - 1), False)])
                    return c
                lax.fori_loop(0, n_rest // 2, rest_body, 0)

    chunk = 256

    def merge_body(c, carry):
        rows = pl.ds(pl.multiple_of(c * chunk, chunk), chunk)
        for p in range(tiles):
            l0, l1, l2 = lg_ref[p, rows, :], lg_ref[tiles + p, rows, :], lg_ref[2 * tiles + p, rows, :]
            m = jnp.maximum(jnp.maximum(l0, l1), l2)
            w0, w1, w2 = jnp.exp2(l0 - m), jnp.exp2(l1 - m), jnp.exp2(l2 - m)
            num = (w0 * og_ref[p, rows, :] + w1 * og_ref[tiles + p, rows, :]
                   + w2 * og_ref[2 * tiles + p, rows, :])
            o_ref[0, rows, p * LANES:(p + 1) * LANES] = (num / (w0 + w1 + w2)).astype(o_ref.dtype)
        return carry

    lax.fori_loop(0, SPAN // chunk, merge_body, 0)


def _no_fill(k, anchor):
    del k, anchor


def _merge_math(x, o_att, c, sa, sb, watt_ref, lnw_ref, lnb_ref, wco_ref, wout_ref, fill=_no_fill):
    a = _dot(o_att.astype(BF16), watt_ref[...])
    fill(1, a)
    mu = jnp.mean(c, axis=-1, keepdims=True)
    xc = c - mu
    var = jnp.mean(xc * xc, axis=-1, keepdims=True)
    y = xc * lax.rsqrt(var + LN_EPS) * lnw_ref[...] + lnb_ref[...]
    act = y * _sigmoid(y)
    cb = _dot(act.astype(BF16), wco_ref[...])
    fill(2, cb)
    h = sa.astype(F32) * a + sb.astype(F32) * cb
    x1 = x + _dot(h.astype(BF16), wout_ref[...])
    fill(2, x1)
    return x1


def _ffn_math(x, nw_ref, wgu_ref, wd_ref, fill=_no_fill):
    xn = _rms_rows(x, nw_ref[...]).astype(BF16)
    acc = x
    for c0 in range(0, D_FF, FF_CHUNK):
        g = _dot(xn, wgu_ref[:, c0:c0 + FF_CHUNK])
        fill(3, g)
        up = _dot(xn, wgu_ref[:, D_FF + c0:D_FF + c0 + FF_CHUNK])
        fill(3, up)
        act = (g * _sigmoid(g) * up).astype(BF16)
        acc = acc + _dot(act, wd_ref[c0:c0 + FF_CHUNK, :])
    return acc


def _merge_ffn_prompt_kernel(x_ref, oa_ref, u_ref, uh_ref, sa_ref, sb_ref, watt_ref, cw_ref, cb_ref,
                             lnw_ref, lnb_ref, wco_ref, wout_ref, nfw_ref, wgu_ref, wd_ref,
                             y_ref, ubuf_ref, cnext_ref, ccur_ref, *, tm, blocks_per_seq, n_blocks):
    n = pl.program_id(0)

    @pl.when(n == 0)
    def _():
        ccur_ref[...] = jnp.zeros(ccur_ref.shape, F32)

    seq_start = lax.rem(jnp.minimum(n, n_blocks - 1), blocks_per_seq) == 0
    for lt in range(LANE_TILES):
        lanes = slice(lt * LANES, (lt + 1) * LANES)
        ubuf_ref[lt, 0:HALO, :] = jnp.where(seq_start, 0.0, uh_ref[:, lanes])
        ubuf_ref[lt, HALO:HALO + tm, :] = u_ref[:, lanes]
    rc = 64
    off = HALO - (CONV_WIDTH - 1)

    def conv_piece(lt, parity, anchor):
        lanes = slice(lt * LANES, (lt + 1) * LANES)
        bits = pltpu.bitcast(anchor[0:rc, 0:LANES], jnp.uint32)
        zero = ((bits >> 16) >> 16).astype(F32)
        for r0 in range(0, tm // 2, rc):
            acc = cb_ref[:, lanes] + zero
            for j in range(CONV_WIDTH):
                start = 2 * r0 + parity + off + j
                acc = acc + ubuf_ref[lt, pl.ds(start, rc, stride=2), :] * cw_ref[j:j + 1, lanes]
            cnext_ref[lt, pl.ds(2 * r0 + parity, rc, stride=2), :] = acc

    pending = [(lt, parity) for lt in range(LANE_TILES) for parity in range(2)]

    def fill(k, anchor):
        for _ in range(min(k, len(pending))):
            conv_piece(*pending.pop(0), anchor)

    c = jnp.concatenate([ccur_ref[lt] for lt in range(LANE_TILES)], axis=1)
    x1 = _merge_math(x_ref[...], oa_ref[...], c, sa_ref[...], sb_ref[...],
                     watt_ref, lnw_ref, lnb_ref, wco_ref, wout_ref, fill)
    y_ref[...] = _ffn_math(x1, nfw_ref, wgu_ref, wd_ref, fill)
    assert not pending
    ccur_ref[...] = cnext_ref[...]


def _merge_ffn_sample_kernel(x_ref, oa_ref, c_ref, sa_ref, sb_ref, watt_ref, lnw_ref, lnb_ref, wco_ref, wout_ref,
                             nfw_ref, wgu_ref, wd_ref, y_ref):
    x1 = _merge_math(x_ref[...], oa_ref[...], c_ref[...], sa_ref[...], sb_ref[...],
                     watt_ref, lnw_ref, lnb_ref, wco_ref, wout_ref)
    y_ref[...] = _ffn_math(x1, nfw_ref, wgu_ref, wd_ref)


def _sample_step_kernel(q_ref, kn_ref, vn_ref, c0_ref, c1_ref, c2_ref, st_ref, u_ref, cw_ref, cb_ref,
                        o_ref, s0_ref, s1_ref, s2_ref, c_ref, cs_ref, ue_ref, new8_ref, *, n_new):
    caches = (c0_ref, c1_ref, c2_ref)
    outs = (s0_ref, s1_ref, s2_ref)
    pad = 8
    row = lax.broadcasted_iota(jnp.int32, (HEADS_PER_GROUP * pad, GROUP_W), 0)
    col = lax.broadcasted_iota(jnp.int32, (HEADS_PER_GROUP * pad, GROUP_W), 1)
    head_lanes = (col >> 6) == (row >> 3)
    new8_ref[...] = jnp.zeros(new8_ref.shape, F32)
    new8_ref[0, 0:n_new, :] = q_ref[0]
    new8_ref[1, 0:n_new, :] = kn_ref[0]
    new8_ref[2, 0:n_new, :] = vn_ref[0]
    o_parts, l_parts = [], []
    for g, (win, dil) in enumerate(zip(WINDOWS, DILATIONS)):
        cols = slice(g * GROUP_W, (g + 1) * GROUP_W)
        cache = caches[g]
        outs[g][0, 0:win - n_new, :] = cache[0, n_new:win, :]
        outs[g][0, win - n_new:win, 0:GROUP_W] = kn_ref[0, :, cols]
        outs[g][0, win - n_new:win, GROUP_W:2 * GROUP_W] = vn_ref[0, :, cols]
        q8 = new8_ref[0, :, cols]
        qm = jnp.where(head_lanes, jnp.concatenate([q8] * HEADS_PER_GROUP, axis=0), 0.0).astype(BF16)
        kc = cache[0, :, 0:GROUP_W].astype(BF16)
        vc = cache[0, :, GROUP_W:2 * GROUP_W].astype(BF16)
        kn8 = new8_ref[1, :, cols].astype(BF16)
        vn8 = new8_ref[2, :, cols].astype(BF16)
        nt = (((1,), (1,)), ((), ()))
        s_c = lax.dot_general(qm, kc, nt, preferred_element_type=F32)
        s_n = lax.dot_general(qm, kn8, nt, preferred_element_type=F32)
        sq_c = lax.broadcasted_iota(jnp.int32, s_c.shape, 0) & (pad - 1)
        kc_i = lax.broadcasted_iota(jnp.int32, s_c.shape, 1)
        d_c = win + sq_c - kc_i
        ok_c = ((d_c & (dil - 1)) == 0) & (d_c <= win)
        sq_n = lax.broadcasted_iota(jnp.int32, s_n.shape, 0) & (pad - 1)
        kn_i = lax.broadcasted_iota(jnp.int32, s_n.shape, 1)
        d_n = sq_n - kn_i
        ok_n = (d_n >= 0) & ((d_n & (dil - 1)) == 0) & (kn_i < n_new)
        s_c = jnp.where(ok_c, s_c, NEG_INF)
        s_n = jnp.where(ok_n, s_n, NEG_INF)
        m = jnp.maximum(jnp.max(s_c, axis=-1, keepdims=True), jnp.max(s_n, axis=-1, keepdims=True))
        p_c = jnp.exp2(s_c - m)
        p_n = jnp.exp2(s_n - m)
        l = jnp.sum(p_c, axis=-1, keepdims=True) + jnp.sum(p_n, axis=-1, keepdims=True)
        acc = _dot(p_c.astype(BF16), vc) + _dot(p_n.astype(BF16), vn8)
        o_parts.append(acc / l)
        l_parts.append(m + jnp.log2(l))
    m = jnp.maximum(jnp.maximum(l_parts[0], l_parts[1]), l_parts[2])
    w = [jnp.exp2(lp - m) for lp in l_parts]
    num = w[0] * o_parts[0] + w[1] * o_parts[1] + w[2] * o_parts[2]
    om = jnp.where(head_lanes, num / (w[0] + w[1] + w[2]), 0.0)
    o8 = om[0:pad] + om[pad:2 * pad] + om[2 * pad:3 * pad] + om[3 * pad:4 * pad]
    o_ref[0] = o8[0:n_new, :]

    n_state = CONV_WIDTH - 1
    ue_ref[0:n_state, :] = st_ref[0]
    ue_ref[n_state:n_state + n_new, :] = u_ref[0]
    ue_ref[n_state + n_new:, :] = jnp.zeros((ue_ref.shape[0] - n_state - n_new, C_CONV), F32)
    acc = jnp.broadcast_to(cb_ref[...], (pad, C_CONV))
    for j in range(CONV_WIDTH):
        acc = acc + ue_ref[j:j + pad, :] * cw_ref[j:j + 1, :]
    c_ref[0] = acc[0:n_new, :]
    cs_ref[0] = ue_ref[n_new:n_new + n_state, :]


def _const_spec(shape, single=False):
    nd = len(shape)
    if single:
        return pl.BlockSpec(shape, lambda *_: (0,) * nd, pipeline_mode=pl.Buffered(1))
    return pl.BlockSpec(shape, lambda *_: (0,) * nd)


def _params(*sem):
    return pltpu.CompilerParams(dimension_semantics=sem, vmem_limit_bytes=VMEM_LIMIT)


def kernel(x_prompt, x_sample, cache_kv_w128, cache_kv_w512, cache_kv_w2048, state_conv, norm_mix_w, w_in,
           q_norm_w, k_norm_w, w_att, conv_w, conv_b, conv_ln_w, conv_ln_b, w_conv_out, w_out, norm_ffn_w,
           w_gate_up, w_down):
    B, T, D = x_prompt.shape
    SB, SS, _ = x_sample.shape
    assert D == D_MODEL and T % SPAN == 0 and norm_mix_w.shape[0] == 1
    caches = (cache_kv_w128, cache_kv_w512, cache_kv_w2048)
    for c, win in zip(caches, WINDOWS):
        assert c.shape[2] == win, "cached window shorter than the attention window is not supported"

    nmw = norm_mix_w.reshape(1, D)
    w_in_b = w_in[0].astype(BF16)
    qw = jnp.tile(q_norm_w[0], D_ATT // HEAD_DIM).reshape(1, D_ATT)
    kw = jnp.tile(k_norm_w[0], D_ATT // HEAD_DIM).reshape(1, D_ATT)
    w_att_b = w_att[0].astype(BF16)
    cw = conv_w[0]
    cb = conv_b.reshape(1, C_CONV)
    lnw = conv_ln_w.reshape(1, C_CONV)
    lnb = conv_ln_b.reshape(1, C_CONV)
    wco_b = w_conv_out[0].astype(BF16)
    wout_b = w_out[0].astype(BF16)
    nfw = norm_ffn_w.reshape(1, D)
    wgu_b = w_gate_up[0].astype(BF16)
    wd_b = w_down[0].astype(BF16)

    tm = 512
    n_t = T // tm
    perm_shapes = [jax.ShapeDtypeStruct((B, dil, T // dil, GROUP_W), BF16) for dil in DILATIONS]
    perm_specs = [pl.BlockSpec((1, dil, tm // dil, GROUP_W), lambda b, i: (b, 0, i, 0)) for dil in DILATIONS]
    tail_rows = [min(win, tm) for win in WINDOWS]
    tail_shapes = [jax.ShapeDtypeStruct((B, win, 2 * GROUP_W), F32) for win in WINDOWS]
    tail_specs = [
        pl.BlockSpec((1, rows, 2 * GROUP_W),
                     functools.partial(lambda b, i, first: (b, jnp.maximum(i - first, 0), 0),
                                       first=n_t - win // rows))
        for win, rows in zip(WINDOWS, tail_rows)]
    tok_spec = pl.BlockSpec((1, tm, D), lambda b, i: (b, i, 0))
    inproj_w_specs = [_const_spec((1, D)), _const_spec((D, N_IN), single=True), _const_spec((1, D_ATT)),
                      _const_spec((1, D_ATT))]
    res = pl.pallas_call(
        functools.partial(_inproj_prompt_kernel, tm=tm, n_t=n_t),
        grid=(B, n_t),
        in_specs=[tok_spec] + inproj_w_specs,
        out_specs=perm_specs * 3 + [tok_spec] * 3 + tail_specs
        + [pl.BlockSpec((1, HALO, C_CONV), lambda b, i: (b, 0, 0))],
        out_shape=perm_shapes * 3 + [jax.ShapeDtypeStruct((B, T, C_CONV), F32)]
        + [jax.ShapeDtypeStruct((B, T, D), BF16)] * 2 + tail_shapes
        + [jax.ShapeDtypeStruct((B, HALO, C_CONV), F32)],
        scratch_shapes=[pltpu.VMEM((D_ATT // LANES, tm, LANES), F32)] * 3,
        compiler_params=_params("arbitrary", "arbitrary"),
        name="inproj_prompt",
    )(x_prompt, nmw, w_in_b, qw, kw)
    qp, kp, vp = res[0:3], res[3:6], res[6:9]
    u_p, sa_p, sb_p = res[9:12]
    tails_p = res[12:15]
    conv_tail_p = res[15]

    n_sb = T // SPAN
    att_in, att_specs = [], []
    for g, dil in enumerate(DILATIONS):
        rows = SPAN // dil
        cur = pl.BlockSpec((1, dil, rows, GROUP_W), lambda b, s: (b, 0, s, 0))
        prev = pl.BlockSpec((1, dil, BAND, GROUP_W),
                            functools.partial(lambda b, s, n: (b, 0, jnp.maximum(s * n - 1, 0), 0),
                                              n=rows // BAND))
        att_in += [qp[g], kp[g], vp[g], kp[g], vp[g]]
        att_specs += [cur, cur, cur, prev, prev]
    o_att_p = pl.pallas_call(
        _attn_prompt_kernel,
        grid=(B, n_sb),
        in_specs=att_specs,
        out_specs=pl.BlockSpec((1, SPAN, GROUP_W), lambda b, s: (b, s, 0)),
        out_shape=jax.ShapeDtypeStruct((B, T, GROUP_W), BF16),
        scratch_shapes=[pltpu.VMEM((N_GROUPS * GROUP_W // LANES, SPAN, LANES), F32)] * 2,
        compiler_params=_params("arbitrary", "arbitrary"),
        name="attn_prompt",
    )(*att_in)

    tm3 = 256
    n_tok = B * T
    n_blocks = n_tok // tm3
    done = lambda n: (jnp.maximum(n - 1, 0), 0)
    ahead = lambda n: (jnp.minimum(n, n_blocks - 1), 0)
    merge_w_specs = [_const_spec((GROUP_W, D), single=True), _const_spec((1, C_CONV)), _const_spec((1, C_CONV)),
                     _const_spec((C_CONV, D), single=True), _const_spec((D, D), single=True)]
    conv_w_specs = [_const_spec((CONV_WIDTH, C_CONV)), _const_spec((1, C_CONV))]
    ffn_w_specs = [_const_spec((1, D)), _const_spec((D, 2 * D_FF), single=True),
                   _const_spec((D_FF, D), single=True)]
    u_flat = u_p.reshape(n_tok, C_CONV)
    y_p = pl.pallas_call(
        functools.partial(_merge_ffn_prompt_kernel, tm=tm3, blocks_per_seq=T // tm3, n_blocks=n_blocks),
        grid=(n_blocks + 1,),
        in_specs=[pl.BlockSpec((tm3, D), done), pl.BlockSpec((tm3, GROUP_W), done),
                  pl.BlockSpec((tm3, C_CONV), ahead),
                  pl.BlockSpec((HALO, C_CONV),
                               lambda n: (jnp.maximum(jnp.minimum(n, n_blocks - 1) * (tm3 // HALO) - 1, 0), 0)),
                  pl.BlockSpec((tm3, D), done), pl.BlockSpec((tm3, D), done), merge_w_specs[0]]
        + conv_w_specs + merge_w_specs[1:] + ffn_w_specs,
        out_specs=pl.BlockSpec((tm3, D), done),
        out_shape=jax.ShapeDtypeStruct((n_tok, D), F32),
        scratch_shapes=[pltpu.VMEM((LANE_TILES, tm3 + HALO, LANES), F32)]
        + [pltpu.VMEM((LANE_TILES, tm3, LANES), F32)] * 2,
        compiler_params=_params("arbitrary"),
        name="merge_ffn_prompt",
    )(x_prompt.reshape(n_tok, D), o_att_p.reshape(n_tok, GROUP_W), u_flat, u_flat, sa_p.reshape(n_tok, D),
      sb_p.reshape(n_tok, D), w_att_b, cw, cb, lnw, lnb, wco_b, wout_b, nfw, wgu_b, wd_b).reshape(B, T, D)

    n_s = SB * SS
    xs2 = x_sample.reshape(n_s, D)
    row_shapes = ([jax.ShapeDtypeStruct((n_s, D_ATT), F32)] * 3 + [jax.ShapeDtypeStruct((n_s, C_CONV), F32)]
                  + [jax.ShapeDtypeStruct((n_s, D), BF16)] * 2)
    q_s, k_s, v_s, u_s, sa_s, sb_s = pl.pallas_call(
        _inproj_sample_kernel,
        grid=(1,),
        in_specs=[_const_spec((n_s, D))] + inproj_w_specs,
        out_specs=[_const_spec(s.shape) for s in row_shapes],
        out_shape=row_shapes,
        compiler_params=_params("arbitrary"),
        name="inproj_sample",
    )(xs2, nmw, w_in_b, qw, kw)

    seq_spec = lambda rows, width: pl.BlockSpec((1, rows, width), lambda b: (b, 0, 0))
    cache2 = [c[0].reshape(SB, win, 2 * GROUP_W) for c, win in zip(caches, WINDOWS)]
    n_state = CONV_WIDTH - 1
    step_out_shapes = ([jax.ShapeDtypeStruct((SB, SS, GROUP_W), F32)]
                       + [jax.ShapeDtypeStruct((SB, win, 2 * GROUP_W), F32) for win in WINDOWS]
                       + [jax.ShapeDtypeStruct((SB, SS, C_CONV), F32),
                          jax.ShapeDtypeStruct((SB, n_state, C_CONV), F32)])
    o_att_s, kv0_s, kv1_s, kv2_s, c_s, conv_state_s = pl.pallas_call(
        functools.partial(_sample_step_kernel, n_new=SS),
        grid=(SB,),
        in_specs=[seq_spec(SS, D_ATT)] * 3 + [seq_spec(win, 2 * GROUP_W) for win in WINDOWS]
        + [seq_spec(n_state, C_CONV), seq_spec(SS, C_CONV)] + conv_w_specs,
        out_specs=[seq_spec(SS, GROUP_W)] + [seq_spec(win, 2 * GROUP_W) for win in WINDOWS]
        + [seq_spec(SS, C_CONV), seq_spec(n_state, C_CONV)],
        out_shape=step_out_shapes,
        scratch_shapes=[pltpu.VMEM((CONV_WIDTH + 9, C_CONV), F32), pltpu.VMEM((3, 8, D_ATT), F32)],
        compiler_params=_params("arbitrary"),
        name="sample_step",
    )(q_s.reshape(SB, SS, D_ATT), k_s.reshape(SB, SS, D_ATT), v_s.reshape(SB, SS, D_ATT), *cache2,
      state_conv[0], u_s.reshape(SB, SS, C_CONV), cw, cb)

    y_s = pl.pallas_call(
        _merge_ffn_sample_kernel,
        grid=(1,),
        in_specs=[_const_spec((n_s, D)), _const_spec((n_s, GROUP_W)), _const_spec((n_s, C_CONV)),
                  _const_spec((n_s, D)), _const_spec((n_s, D))] + merge_w_specs + ffn_w_specs,
        out_specs=_const_spec((n_s, D)),
        out_shape=jax.ShapeDtypeStruct((n_s, D), F32),
        compiler_params=_params("arbitrary"),
        name="merge_ffn_sample",
    )(xs2, o_att_s.reshape(n_s, GROUP_W), c_s.reshape(n_s, C_CONV), sa_s, sb_s, w_att_b, lnw, lnb, wco_b, wout_b,
      nfw, wgu_b, wd_b).reshape(SB, SS, D)

    def kv_shape(a, nb, win):
        return a.reshape(1, nb, win, 2, HEADS_PER_GROUP, HEAD_DIM)

    return (y_p, y_s,
            kv_shape(tails_p[0], B, WINDOWS[0]), kv_shape(kv0_s, SB, WINDOWS[0]),
            kv_shape(tails_p[1], B, WINDOWS[1]), kv_shape(kv1_s, SB, WINDOWS[1]),
            kv_shape(tails_p[2], B, WINDOWS[2]), kv_shape(kv2_s, SB, WINDOWS[2]),
            conv_tail_p[:, HALO - n_state:, :][None], conv_state_s[None])
```

```python
import functools
import math

import jax
import jax.numpy as jnp
from jax import lax
from jax.experimental import pallas as pl
from jax.experimental.pallas import tpu as pltpu

F32 = jnp.float32
BF16 = jnp.bfloat16

D_MODEL = 1024
HEAD_DIM = 64
HEADS_PER_GROUP = 4
GROUP_W = HEADS_PER_GROUP * HEAD_DIM
N_GROUPS = 3
D_ATT = N_GROUPS * GROUP_W
WINDOWS = (128, 512, 2048)
DILATIONS = (1, 4, 16)
BAND = 128
SPAN = 2048
C_CONV = D_MODEL
CONV_WIDTH = 31
HALO = 32
D_FF = 2816
FF_CHUNK = 1408
N_IN = 3 * D_ATT + 2 * C_CONV + 2 * D_MODEL
RMS_EPS = 1e-6
LN_EPS = 1e-5
NEG_INF = -1e30
ATT_SCALE = HEAD_DIM ** -0.5
Q_SCALE = ATT_SCALE * math.log2(math.e)
LANES = 128
LANE_TILES = C_CONV // LANES
VMEM_LIMIT = 56 * 1024 * 1024

O_Q, O_K, O_V = 0, D_ATT, 2 * D_ATT
O_UA = 3 * D_ATT
O_UB = O_UA + C_CONV
O_GA = O_UB + C_CONV
O_GB = O_GA + D_MODEL


def _dot(a, b):
    return jnp.dot(a, b, preferred_element_type=F32)


def _sigmoid(x):
    return 1.0 / (1.0 + jnp.exp(-x))


def _rms_rows(x, w):
    ms = jnp.mean(x * x, axis=-1, keepdims=True)
    return x * lax.rsqrt(ms + RMS_EPS) * w


def _head_norm_tiles(z, w_ref, scale):
    rows = z.shape[0]
    lane = lax.broadcasted_iota(jnp.int32, (rows, LANES), 1)
    low = lane < HEAD_DIM
    tiles = []
    for c in range(D_ATT // LANES):
        x = z[:, c * LANES:(c + 1) * LANES]
        x2 = x * x
        s_lo = jnp.sum(jnp.where(low, x2, 0.0), axis=-1, keepdims=True)
        s_hi = jnp.sum(jnp.where(low, 0.0, x2), axis=-1, keepdims=True)
        ms = jnp.where(low, s_lo, s_hi) * (1.0 / HEAD_DIM)
        y = x * lax.rsqrt(ms + RMS_EPS) * w_ref[:, c * LANES:(c + 1) * LANES]
        tiles.append(y * scale if scale != 1.0 else y)
    return tiles


def _in_projection(x, nw_ref, w_ref, qw_ref, kw_ref, store_qkv):
    xn = _rms_rows(x, nw_ref[...]).astype(BF16)
    for c, t in enumerate(_head_norm_tiles(_dot(xn, w_ref[:, O_Q:O_Q + D_ATT]), qw_ref, Q_SCALE)):
        store_qkv(0, c, t)
    for c, t in enumerate(_head_norm_tiles(_dot(xn, w_ref[:, O_K:O_K + D_ATT]), kw_ref, 1.0)):
        store_qkv(1, c, t)
    v = _dot(xn, w_ref[:, O_V:O_V + D_ATT])
    for c in range(D_ATT // LANES):
        store_qkv(2, c, v[:, c * LANES:(c + 1) * LANES])
    ua = _dot(xn, w_ref[:, O_UA:O_UA + C_CONV])
    ub = _dot(xn, w_ref[:, O_UB:O_UB + C_CONV])
    u = ua * _sigmoid(ub)
    sa = _sigmoid(_dot(xn, w_ref[:, O_GA:O_GA + D_MODEL]))
    sb = _sigmoid(_dot(xn, w_ref[:, O_GB:O_GB + D_MODEL]))
    return u, sa, sb


def _inproj_prompt_kernel(x_ref, nw_ref, w_ref, qw_ref, kw_ref,
                          q0, q1, q2, k0, k1, k2, v0, v1, v2, u_ref, sa_ref, sb_ref,
                          t0, t1, t2, ct_ref, qs_ref, ks_ref, vs_ref, *, tm, n_t):
    i = pl.program_id(1)
    stage = (qs_ref, ks_ref, vs_ref)

    def store_qkv(which, c, tile):
        stage[which][c] = tile

    u, sa, sb = _in_projection(x_ref[0], nw_ref, w_ref, qw_ref, kw_ref, store_qkv)
    u_ref[0] = u
    sa_ref[0] = sa.astype(sa_ref.dtype)
    sb_ref[0] = sb.astype(sb_ref.dtype)
    tiles_per_group = GROUP_W // LANES
    for src, outs in ((qs_ref, (q0, q1, q2)), (ks_ref, (k0, k1, k2)), (vs_ref, (v0, v1, v2))):
        for g, dil in enumerate(DILATIONS):
            for p in range(tiles_per_group):
                c = g * tiles_per_group + p
                lanes = slice(p * LANES, (p + 1) * LANES)
                if dil == 1:
                    outs[g][0, 0, :, lanes] = src[c].astype(BF16)
                else:
                    for r in range(dil):
                        outs[g][0, r, :, lanes] = src[c, pl.ds(r, tm // dil, stride=dil), :].astype(BF16)
    for g, (win, t_ref) in enumerate(zip(WINDOWS, (t0, t1, t2))):
        rows = min(win, tm)
        first = n_t - win // rows

        @pl.when(i >= first)
        def _(t_ref=t_ref, g=g, rows=rows):
            for p in range(tiles_per_group):
                c = g * tiles_per_group + p
                t_ref[0, :, p * LANES:(p + 1) * LANES] = ks_ref[c, tm - rows:tm, :]
                t_ref[0, :, GROUP_W + p * LANES:GROUP_W + (p + 1) * LANES] = vs_ref[c, tm - rows:tm, :]

    @pl.when(i == n_t - 1)
    def _():
        ct_ref[0] = u[tm - HALO:tm, :]


def _inproj_sample_kernel(x_ref, nw_ref, w_ref, qw_ref, kw_ref, q_ref, k_ref, v_ref, u_ref, sa_ref, sb_ref):
    outs = (q_ref, k_ref, v_ref)

    def store_qkv(which, c, tile):
        outs[which][:, c * LANES:(c + 1) * LANES] = tile

    u, sa, sb = _in_projection(x_ref[...], nw_ref, w_ref, qw_ref, kw_ref, store_qkv)
    for lt in range(LANE_TILES):
        u_ref[lt] = u[:, lt * LANES:(lt + 1) * LANES]
    sa_ref[...] = sa.astype(sa_ref.dtype)
    sb_ref[...] = sb.astype(sb_ref.dtype)


def _attn_unit(qs, kks, vvs, lo_limits):
    qi = lax.broadcasted_iota(jnp.int32, (2 * BAND, 2 * BAND), 0) & (BAND - 1)
    ki = lax.broadcasted_iota(jnp.int32, (2 * BAND, 2 * BAND), 1)
    band = (ki >= qi) & (ki <= qi + BAND)
    low = lax.broadcasted_iota(jnp.int32, (BAND, LANES), 1) < HEAD_DIM
    scores = []
    for q2, kk in zip(qs, kks):
        zero = jnp.zeros_like(q2)
        q_st = jnp.concatenate([jnp.where(low, q2, zero), jnp.where(low, zero, q2)], axis=0)
        scores.append(lax.dot_general(q_st, kk, (((1,), (1,)), ((), ())), preferred_element_type=F32))
    res = []
    for s, vv, lo in zip(scores, vvs, lo_limits):
        valid = band if isinstance(lo, int) and lo == 0 else band & (ki >= lo)
        s = jnp.where(valid, s, NEG_INF)
        m = jnp.max(s, axis=-1, keepdims=True)
        p = jnp.exp2(s - m)
        l = jnp.sum(p, axis=-1, keepdims=True)
        o = _dot(p.astype(BF16), vv) / l
        lse = m + jnp.log2(l)
        res.append((jnp.where(low, o[0:BAND], o[BAND:2 * BAND]),
                    jnp.where(low, lse[0:BAND], lse[BAND:2 * BAND])))
    return res


def _attn_prompt_kernel(q0, k0, v0, kp0, vp0, q1, k1, v1, kp1, vp1, q2, k2, v2, kp2, vp2,
                        o_ref, og_ref, lg_ref):
    sb = pl.program_id(1)
    refs = ((q0, k0, v0, kp0, vp0), (q1, k1, v1, kp1, vp1), (q2, k2, v2, kp2, vp2))
    first_lo = jnp.where(sb > 0, 0, BAND)
    tiles = GROUP_W // LANES

    for g in range(N_GROUPS):
        dil, win = DILATIONS[g], WINDOWS[g]
        q_ref, k_ref, v_ref, kp_ref, vp_ref = refs[g]

        def units(blocks, g=g, dil=dil, win=win, q_ref=q_ref, k_ref=k_ref, v_ref=v_ref,
                  kp_ref=kp_ref, vp_ref=vp_ref):
            qs, kks, vvs, los, dst = [], [], [], [], []
            for seg, r, first in blocks:
                row0 = seg * BAND
                if not isinstance(row0, int):
                    row0 = pl.multiple_of(row0, BAND)
                start = seg * win + r
                rows = pl.ds(start, BAND) if dil == 1 else pl.ds(start, BAND, stride=dil)
                for p in range(tiles):
                    lanes = slice(p * LANES, (p + 1) * LANES)
                    qs.append(q_ref[0, r, pl.ds(row0, BAND), lanes])
                    if first:
                        kks.append(jnp.concatenate([kp_ref[0, r, :, lanes], k_ref[0, r, 0:BAND, lanes]], axis=0))
                        vvs.append(jnp.concatenate([vp_ref[0, r, :, lanes], v_ref[0, r, 0:BAND, lanes]], axis=0))
                        los.append(first_lo)
                    else:
                        kks.append(k_ref[0, r, pl.ds(row0 - BAND, 2 * BAND), lanes])
                        vvs.append(v_ref[0, r, pl.ds(row0 - BAND, 2 * BAND), lanes])
                        los.append(0)
                    dst.append((g * tiles + p, rows))
            for (slot, rows), (o2, l2) in zip(dst, _attn_unit(qs, kks, vvs, los)):
                og_ref[slot, rows, :] = o2
                lg_ref[slot, rows, :] = l2

        shift = dil.bit_length() - 1
        per = 4
        if dil == 1:
            units([(0, 0, True), (1, 0, False), (2, 0, False), (3, 0, False)])

            def rest_body(t, c, units=units):
                units([(per + per * t + k, 0, False) for k in range(per)])
                return c
            lax.fori_loop(0, (SPAN // win - per) // per, rest_body, 0)
        else:
            def first_body(t, c, units=units):
                units([(0, per * t + k, True) for k in range(per)])
                return c
            lax.fori_loop(0, dil // per, first_body, 0)
            n_rest = (SPAN // win - 1) * dil
            if n_rest:
                def rest_body(t, c, units=units, dil=dil, shift=shift):
                    idx = [per * t + k for k in range(per)]
                    units([(1 + (a >> shift), a & (dil - 1), False) for a in idx])
                    return c
                lax.fori_loop(0, n_rest // per, rest_body, 0)

    chunk = 256

    def merge_body(c, carry):
        rows = pl.ds(pl.multiple_of(c * chunk, chunk), chunk)
        for p in range(tiles):
            l0, l1, l2 = lg_ref[p, rows, :], lg_ref[tiles + p, rows, :], lg_ref[2 * tiles + p, rows, :]
            m = jnp.maximum(jnp.maximum(l0, l1), l2)
            w0, w1, w2 = jnp.exp2(l0 - m), jnp.exp2(l1 - m), jnp.exp2(l2 - m)
            num = (w0 * og_ref[p, rows, :] + w1 * og_ref[tiles + p, rows, :]
                   + w2 * og_ref[2 * tiles + p, rows, :])
            o_ref[0, rows, p * LANES:(p + 1) * LANES] = (num / (w0 + w1 + w2)).astype(o_ref.dtype)
        return carry

    lax.fori_loop(0, SPAN // chunk, merge_body, 0)


def _no_fill(k, anchor):
    del k, anchor


def _merge_math(x, o_att, c, sa, sb, watt_ref, lnw_ref, lnb_ref, wco_ref, wout_ref, fill=_no_fill):
    a = _dot(o_att.astype(BF16), watt_ref[...])
    fill(1, a)
    mu = jnp.mean(c, axis=-1, keepdims=True)
    xc = c - mu
    var = jnp.mean(xc * xc, axis=-1, keepdims=True)
    y = xc * lax.rsqrt(var + LN_EPS) * lnw_ref[...] + lnb_ref[...]
    act = y * _sigmoid(y)
    cb = _dot(act.astype(BF16), wco_ref[...])
    fill(2, cb)
    h = sa.astype(F32) * a + sb.astype(F32) * cb
    x1 = x + _dot(h.astype(BF16), wout_ref[...])
    fill(2, x1)
    return x1


def _ffn_math(x, nw_ref, wgu_ref, wd_ref, fill=_no_fill):
    xn = _rms_rows(x, nw_ref[...]).astype(BF16)
    acc = x
    for c0 in range(0, D_FF, FF_CHUNK):
        g = _dot(xn, wgu_ref[:, c0:c0 + FF_CHUNK])
        fill(3, g)
        up = _dot(xn, wgu_ref[:, D_FF + c0:D_FF + c0 + FF_CHUNK])
        fill(3, up)
        act = (g * _sigmoid(g) * up).astype(BF16)
        acc = acc + _dot(act, wd_ref[c0:c0 + FF_CHUNK, :])
    return acc


def _merge_ffn_prompt_kernel(x_ref, oa_ref, u_ref, uh_ref, sa_ref, sb_ref, watt_ref, cw_ref, cb_ref,
                             lnw_ref, lnb_ref, wco_ref, wout_ref, nfw_ref, wgu_ref, wd_ref,
                             y_ref, ubuf_ref, cnext_ref, ccur_ref, *, tm, blocks_per_seq, n_blocks):
    n = pl.program_id(0)

    @pl.when(n == 0)
    def _():
        ccur_ref[...] = jnp.zeros(ccur_ref.shape, F32)

    seq_start = lax.rem(jnp.minimum(n, n_blocks - 1), blocks_per_seq) == 0
    for lt in range(LANE_TILES):
        lanes = slice(lt * LANES, (lt + 1) * LANES)
        ubuf_ref[lt, 0:HALO, :] = jnp.where(seq_start, 0.0, uh_ref[:, lanes])
        ubuf_ref[lt, HALO:HALO + tm, :] = u_ref[:, lanes]
    rc = 64
    off = HALO - (CONV_WIDTH - 1)

    def conv_piece(lt, parity, anchor):
        lanes = slice(lt * LANES, (lt + 1) * LANES)
        bits = pltpu.bitcast(anchor[0:rc, 0:LANES], jnp.uint32)
        zero = ((bits >> 16) >> 16).astype(F32)
        for r0 in range(0, tm // 2, rc):
            acc = cb_ref[:, lanes] + zero
            for j in range(CONV_WIDTH):
                start = 2 * r0 + parity + off + j
                acc = acc + ubuf_ref[lt, pl.ds(start, rc, stride=2), :] * cw_ref[j:j + 1, lanes]
            cnext_ref[lt, pl.ds(2 * r0 + parity, rc, stride=2), :] = acc

    pending = [(lt, parity) for lt in range(LANE_TILES) for parity in range(2)]

    def fill(k, anchor):
        for _ in range(min(k, len(pending))):
            conv_piece(*pending.pop(0), anchor)

    c = jnp.concatenate([ccur_ref[lt] for lt in range(LANE_TILES)], axis=1)
    x1 = _merge_math(x_ref[...], oa_ref[...], c, sa_ref[...], sb_ref[...],
                     watt_ref, lnw_ref, lnb_ref, wco_ref, wout_ref, fill)
    y_ref[...] = _ffn_math(x1, nfw_ref, wgu_ref, wd_ref, fill)
    assert not pending
    ccur_ref[...] = cnext_ref[...]


def _merge_ffn_sample_kernel(x_ref, oa_ref, ut_ref, st_ref, sa_ref, sb_ref, cw_ref, cb_ref, watt_ref, lnw_ref,
                             lnb_ref, wco_ref, wout_ref, nfw_ref, wgu_ref, wd_ref, y_ref, cs_ref, c_scr,
                             *, n_new, n_seq):
    n_state = CONV_WIDTH - 1

    def ext_rows(k, lt):
        if k < n_state:
            return st_ref[k, :, lt * LANES:(lt + 1) * LANES]
        return ut_ref[lt, pl.ds(k - n_state, n_seq, stride=n_new), :]

    for lt in range(LANE_TILES):
        lanes = slice(lt * LANES, (lt + 1) * LANES)
        for s in range(n_new):
            acc = jnp.broadcast_to(cb_ref[:, lanes], (n_seq, LANES))
            for j in range(CONV_WIDTH):
                acc = acc + ext_rows(s + j, lt) * cw_ref[j:j + 1, lanes]
            c_scr[lt, pl.ds(s, n_seq, stride=n_new), :] = acc
        for k in range(n_state):
            cs_ref[k, :, lanes] = ext_rows(k + n_new, lt)
    c = jnp.concatenate([c_scr[lt] for lt in range(LANE_TILES)], axis=1)
    x1 = _merge_math(x_ref[...], oa_ref[...], c, sa_ref[...], sb_ref[...],
                     watt_ref, lnw_ref, lnb_ref, wco_ref, wout_ref)
    y_ref[...] = _ffn_math(x1, nfw_ref, wgu_ref, wd_ref)


def _shift_in(x, tail, n_new):
    rows, width = x.shape
    lane = lax.broadcasted_iota(jnp.int32, (rows, LANES), 1)
    keep = lane < LANES - n_new
    rolled = [pltpu.roll(x[:, t * LANES:(t + 1) * LANES], LANES - n_new, axis=1) for t in range(width // LANES)]
    rolled.append(tail)
    return jnp.concatenate([jnp.where(keep, rolled[t], rolled[t + 1]) for t in range(width // LANES)], axis=1)


def _sample_step_kernel(q_ref, kn_ref, vn_ref, knt_ref, vnt_ref, c0_ref, c1_ref, c2_ref,
                        o_ref, s0_ref, s1_ref, s2_ref, new8_ref, tail_ref, *, n_new):
    caches = (c0_ref, c1_ref, c2_ref)
    outs = (s0_ref, s1_ref, s2_ref)
    pad = 8
    row = lax.broadcasted_iota(jnp.int32, (HEADS_PER_GROUP * pad, GROUP_W), 0)
    col = lax.broadcasted_iota(jnp.int32, (HEADS_PER_GROUP * pad, GROUP_W), 1)
    head_lanes = (col >> 6) == (row >> 3)
    new8_ref[...] = jnp.zeros(new8_ref.shape, F32)
    new8_ref[0, 0:n_new, :] = q_ref[0]
    new8_ref[1, 0:n_new, :] = kn_ref[0]
    new8_ref[2, 0:n_new, :] = vn_ref[0]
    tail_ref[...] = jnp.zeros(tail_ref.shape, F32)
    o_parts, l_parts = [], []
    for g, (win, dil) in enumerate(zip(WINDOWS, DILATIONS)):
        cols = slice(g * GROUP_W, (g + 1) * GROUP_W)
        kt = caches[g][0, 0]
        vt = caches[g][0, 1]
        for which, (xt, nt_ref) in enumerate(((kt, knt_ref), (vt, vnt_ref))):
            tail_ref[:, 0:n_new] = nt_ref[0, g]
            tail = pltpu.roll(tail_ref[...], LANES - n_new, axis=1)
            outs[g][0, which] = _shift_in(xt, tail, n_new)
        q8 = new8_ref[0, :, cols]
        qm = jnp.where(head_lanes, jnp.concatenate([q8] * HEADS_PER_GROUP, axis=0), 0.0).astype(BF16)
        kn8 = new8_ref[1, :, cols].astype(BF16)
        vn8 = new8_ref[2, :, cols].astype(BF16)
        nt = (((1,), (1,)), ((), ()))
        s_c = _dot(qm, kt.astype(BF16))
        s_n = lax.dot_general(qm, kn8, nt, preferred_element_type=F32)
        sq_c = lax.broadcasted_iota(jnp.int32, s_c.shape, 0) & (pad - 1)
        kc_i = lax.broadcasted_iota(jnp.int32, s_c.shape, 1)
        d_c = win + sq_c - kc_i
        ok_c = ((d_c & (dil - 1)) == 0) & (d_c <= win)
        sq_n = lax.broadcasted_iota(jnp.int32, s_n.shape, 0) & (pad - 1)
        kn_i = lax.broadcasted_iota(jnp.int32, s_n.shape, 1)
        d_n = sq_n - kn_i
        ok_n = (d_n >= 0) & ((d_n & (dil - 1)) == 0) & (kn_i < n_new)
        s_c = jnp.where(ok_c, s_c, NEG_INF)
        s_n = jnp.where(ok_n, s_n, NEG_INF)
        m = jnp.maximum(jnp.max(s_c, axis=-1, keepdims=True), jnp.max(s_n, axis=-1, keepdims=True))
        p_c = jnp.exp2(s_c - m)
        p_n = jnp.exp2(s_n - m)
        l = jnp.sum(p_c, axis=-1, keepdims=True) + jnp.sum(p_n, axis=-1, keepdims=True)
        acc = (lax.dot_general(p_c.astype(BF16), vt.astype(BF16), nt, preferred_element_type=F32)
               + _dot(p_n.astype(BF16), vn8))
        o_parts.append(acc / l)
        l_parts.append(m + jnp.log2(l))
    m = jnp.maximum(jnp.maximum(l_parts[0], l_parts[1]), l_parts[2])
    w = [jnp.exp2(lp - m) for lp in l_parts]
    num = w[0] * o_parts[0] + w[1] * o_parts[1] + w[2] * o_parts[2]
    om = jnp.where(head_lanes, num / (w[0] + w[1] + w[2]), 0.0)
    o8 = om[0:pad] + om[pad:2 * pad] + om[2 * pad:3 * pad] + om[3 * pad:4 * pad]
    o_ref[0] = o8[0:n_new, :]


def _const_spec(shape, single=False):
    nd = len(shape)
    if single:
        return pl.BlockSpec(shape, lambda *_: (0,) * nd, pipeline_mode=pl.Buffered(1))
    return pl.BlockSpec(shape, lambda *_: (0,) * nd)


def _params(*sem):
    return pltpu.CompilerParams(dimension_semantics=sem, vmem_limit_bytes=VMEM_LIMIT)


def kernel(x_prompt, x_sample, cache_kv_w128, cache_kv_w512, cache_kv_w2048, state_conv, norm_mix_w, w_in,
           q_norm_w, k_norm_w, w_att, conv_w, conv_b, conv_ln_w, conv_ln_b, w_conv_out, w_out, norm_ffn_w,
           w_gate_up, w_down):
    B, T, D = x_prompt.shape
    SB, SS, _ = x_sample.shape
    assert D == D_MODEL and T % SPAN == 0 and norm_mix_w.shape[0] == 1
    caches = (cache_kv_w128, cache_kv_w512, cache_kv_w2048)
    for c, win in zip(caches, WINDOWS):
        assert c.shape[2] == win, "cached window shorter than the attention window is not supported"

    nmw = norm_mix_w.reshape(1, D)
    w_in_b = w_in[0].astype(BF16)
    qw = jnp.tile(q_norm_w[0], D_ATT // HEAD_DIM).reshape(1, D_ATT)
    kw = jnp.tile(k_norm_w[0], D_ATT // HEAD_DIM).reshape(1, D_ATT)
    w_att_b = w_att[0].astype(BF16)
    cw = conv_w[0]
    cb = conv_b.reshape(1, C_CONV)
    lnw = conv_ln_w.reshape(1, C_CONV)
    lnb = conv_ln_b.reshape(1, C_CONV)
    wco_b = w_conv_out[0].astype(BF16)
    wout_b = w_out[0].astype(BF16)
    nfw = norm_ffn_w.reshape(1, D)
    wgu_b = w_gate_up[0].astype(BF16)
    wd_b = w_down[0].astype(BF16)

    tm = 256
    n_t = T // tm
    perm_shapes = [jax.ShapeDtypeStruct((B, dil, T // dil, GROUP_W), BF16) for dil in DILATIONS]
    perm_specs = [pl.BlockSpec((1, dil, tm // dil, GROUP_W), lambda b, i: (b, 0, i, 0)) for dil in DILATIONS]
    tail_rows = [min(win, tm) for win in WINDOWS]
    tail_shapes = [jax.ShapeDtypeStruct((B, win, 2 * GROUP_W), F32) for win in WINDOWS]
    tail_specs = [
        pl.BlockSpec((1, rows, 2 * GROUP_W),
                     functools.partial(lambda b, i, first: (b, jnp.maximum(i - first, 0), 0),
                                       first=n_t - win // rows))
        for win, rows in zip(WINDOWS, tail_rows)]
    tok_spec = pl.BlockSpec((1, tm, D), lambda b, i: (b, i, 0))
    inproj_w_specs = [_const_spec((1, D)), _const_spec((D, N_IN), single=True), _const_spec((1, D_ATT)),
                      _const_spec((1, D_ATT))]
    res = pl.pallas_call(
        functools.partial(_inproj_prompt_kernel, tm=tm, n_t=n_t),
        grid=(B, n_t),
        in_specs=[tok_spec] + inproj_w_specs,
        out_specs=perm_specs * 3 + [tok_spec] * 3 + tail_specs
        + [pl.BlockSpec((1, HALO, C_CONV), lambda b, i: (b, 0, 0))],
        out_shape=perm_shapes * 3 + [jax.ShapeDtypeStruct((B, T, C_CONV), F32)]
        + [jax.ShapeDtypeStruct((B, T, D), BF16)] * 2 + tail_shapes
        + [jax.ShapeDtypeStruct((B, HALO, C_CONV), F32)],
        scratch_shapes=[pltpu.VMEM((D_ATT // LANES, tm, LANES), F32)] * 3,
        compiler_params=_params("arbitrary", "arbitrary"),
        name="inproj_prompt",
    )(x_prompt, nmw, w_in_b, qw, kw)
    qp, kp, vp = res[0:3], res[3:6], res[6:9]
    u_p, sa_p, sb_p = res[9:12]
    tails_p = res[12:15]
    conv_tail_p = res[15]

    n_sb = T // SPAN
    att_in, att_specs = [], []
    for g, dil in enumerate(DILATIONS):
        rows = SPAN // dil
        cur = pl.BlockSpec((1, dil, rows, GROUP_W), lambda b, s: (b, 0, s, 0))
        prev = pl.BlockSpec((1, dil, BAND, GROUP_W),
                            functools.partial(lambda b, s, n: (b, 0, jnp.maximum(s * n - 1, 0), 0),
                                              n=rows // BAND))
        att_in += [qp[g], kp[g], vp[g], kp[g], vp[g]]
        att_specs += [cur, cur, cur, prev, prev]
    o_att_p = pl.pallas_call(
        _attn_prompt_kernel,
        grid=(B, n_sb),
        in_specs=att_specs,
        out_specs=pl.BlockSpec((1, SPAN, GROUP_W), lambda b, s: (b, s, 0)),
        out_shape=jax.ShapeDtypeStruct((B, T, GROUP_W), BF16),
        scratch_shapes=[pltpu.VMEM((N_GROUPS * GROUP_W // LANES, SPAN, LANES), F32)] * 2,
        compiler_params=_params("arbitrary", "arbitrary"),
        name="attn_prompt",
    )(*att_in)

    tm3 = 256
    n_tok = B * T
    n_blocks = n_tok // tm3
    done = lambda n: (jnp.maximum(n - 1, 0), 0)
    ahead = lambda n: (jnp.minimum(n, n_blocks - 1), 0)
    merge_w_specs = [_const_spec((GROUP_W, D), single=True), _const_spec((1, C_CONV)), _const_spec((1, C_CONV)),
                     _const_spec((C_CONV, D), single=True), _const_spec((D, D), single=True)]
    conv_w_specs = [_const_spec((CONV_WIDTH, C_CONV)), _const_spec((1, C_CONV))]
    ffn_w_specs = [_const_spec((1, D)), _const_spec((D, 2 * D_FF), single=True),
                   _const_spec((D_FF, D), single=True)]
    u_flat = u_p.reshape(n_tok, C_CONV)
    y_p = pl.pallas_call(
        functools.partial(_merge_ffn_prompt_kernel, tm=tm3, blocks_per_seq=T // tm3, n_blocks=n_blocks),
        grid=(n_blocks + 1,),
        in_specs=[pl.BlockSpec((tm3, D), done), pl.BlockSpec((tm3, GROUP_W), done),
                  pl.BlockSpec((tm3, C_CONV), ahead),
                  pl.BlockSpec((HALO, C_CONV),
                               lambda n: (jnp.maximum(jnp.minimum(n, n_blocks - 1) * (tm3 // HALO) - 1, 0), 0)),
                  pl.BlockSpec((tm3, D), done), pl.BlockSpec((tm3, D), done), merge_w_specs[0]]
        + conv_w_specs + merge_w_specs[1:] + ffn_w_specs,
        out_specs=pl.BlockSpec((tm3, D), done),
        out_shape=jax.ShapeDtypeStruct((n_tok, D), F32),
        scratch_shapes=[pltpu.VMEM((LANE_TILES, tm3 + HALO, LANES), F32)]
        + [pltpu.VMEM((LANE_TILES, tm3, LANES), F32)] * 2,
        compiler_params=_params("arbitrary"),
        name="merge_ffn_prompt",
    )(x_prompt.reshape(n_tok, D), o_att_p.reshape(n_tok, GROUP_W), u_flat, u_flat, sa_p.reshape(n_tok, D),
      sb_p.reshape(n_tok, D), w_att_b, cw, cb, lnw, lnb, wco_b, wout_b, nfw, wgu_b, wd_b).reshape(B, T, D)

    n_s = SB * SS
    n_state = CONV_WIDTH - 1
    xs2 = x_sample.reshape(n_s, D)
    row_shapes = ([jax.ShapeDtypeStruct((n_s, D_ATT), F32)] * 3
                  + [jax.ShapeDtypeStruct((LANE_TILES, n_s, LANES), F32)]
                  + [jax.ShapeDtypeStruct((n_s, D), BF16)] * 2)
    q_s, k_s, v_s, ut_s, sa_s, sb_s = pl.pallas_call(
        _inproj_sample_kernel,
        grid=(1,),
        in_specs=[_const_spec((n_s, D))] + inproj_w_specs,
        out_specs=[_const_spec(s.shape) for s in row_shapes],
        out_shape=row_shapes,
        compiler_params=_params("arbitrary"),
        name="inproj_sample",
    )(xs2, nmw, w_in_b, qw, kw)

    seq_spec = lambda *dims: pl.BlockSpec((1,) + dims, lambda b: (b,) + (0,) * len(dims))
    cache_t = [jnp.transpose(c[0], (0, 2, 3, 4, 1)).reshape(SB, 2, GROUP_W, win) for c, win in zip(caches, WINDOWS)]
    to_cols = lambda a: a.reshape(SB, SS, N_GROUPS, GROUP_W).transpose(0, 2, 3, 1)
    win_shapes = [jax.ShapeDtypeStruct((SB, 2, GROUP_W, win), F32) for win in WINDOWS]
    win_specs = [seq_spec(2, GROUP_W, win) for win in WINDOWS]
    o_att_s, kv0_s, kv1_s, kv2_s = pl.pallas_call(
        functools.partial(_sample_step_kernel, n_new=SS),
        grid=(SB,),
        in_specs=[seq_spec(SS, D_ATT)] * 3 + [seq_spec(N_GROUPS, GROUP_W, SS)] * 2 + win_specs,
        out_specs=[seq_spec(SS, GROUP_W)] + win_specs,
        out_shape=[jax.ShapeDtypeStruct((SB, SS, GROUP_W), F32)] + win_shapes,
        scratch_shapes=[pltpu.VMEM((3, 8, D_ATT), F32), pltpu.VMEM((GROUP_W, LANES), F32)],
        compiler_params=_params("arbitrary"),
        name="sample_step",
    )(q_s.reshape(SB, SS, D_ATT), k_s.reshape(SB, SS, D_ATT), v_s.reshape(SB, SS, D_ATT), to_cols(k_s), to_cols(v_s),
      *cache_t)

    state_t = jnp.transpose(state_conv[0], (1, 0, 2))
    y_s, conv_state_t = pl.pallas_call(
        functools.partial(_merge_ffn_sample_kernel, n_new=SS, n_seq=SB),
        grid=(1,),
        in_specs=[_const_spec((n_s, D)), _const_spec((n_s, GROUP_W)), _const_spec((LANE_TILES, n_s, LANES)),
                  _const_spec((n_state, SB, C_CONV)), _const_spec((n_s, D)), _const_spec((n_s, D))]
        + conv_w_specs + merge_w_specs + ffn_w_specs,
        out_specs=[_const_spec((n_s, D)), _const_spec((n_state, SB, C_CONV))],
        out_shape=[jax.ShapeDtypeStruct((n_s, D), F32), jax.ShapeDtypeStruct((n_state, SB, C_CONV), F32)],
        scratch_shapes=[pltpu.VMEM((LANE_TILES, n_s, LANES), F32)],
        compiler_params=_params("arbitrary"),
        name="merge_ffn_sample",
    )(xs2, o_att_s.reshape(n_s, GROUP_W), ut_s, state_t, sa_s, sb_s, cw, cb, w_att_b, lnw, lnb, wco_b, wout_b,
      nfw, wgu_b, wd_b)

    def kv_rows(a, nb, win):
        return a.reshape(1, nb, win, 2, HEADS_PER_GROUP, HEAD_DIM)

    def kv_cols(a, nb, win):
        return a.reshape(nb, 2, HEADS_PER_GROUP, HEAD_DIM, win).transpose(0, 4, 1, 2, 3)[None]

    return (y_p, y_s.reshape(SB, SS, D),
            kv_rows(tails_p[0], B, WINDOWS[0]), kv_cols(kv0_s, SB, WINDOWS[0]),
            kv_rows(tails_p[1], B, WINDOWS[1]), kv_cols(kv1_s, SB, WINDOWS[1]),
            kv_rows(tails_p[2], B, WINDOWS[2]), kv_cols(kv2_s, SB, WINDOWS[2]),
            conv_tail_p[:, HALO - n_state:, :][None], jnp.transpose(conv_state_t, (1, 0, 2))[None])
```

```python
import functools
import math

import jax
import jax.numpy as jnp
from jax import lax
from jax.experimental import pallas as pl
from jax.experimental.pallas import tpu as pltpu

F32 = jnp.float32
BF16 = jnp.bfloat16

D_MODEL = 1024
HEAD_DIM = 64
HEADS_PER_GROUP = 4
GROUP_W = HEADS_PER_GROUP * HEAD_DIM
N_GROUPS = 3
D_ATT = N_GROUPS * GROUP_W
WINDOWS = (128, 512, 2048)
DILATIONS = (1, 4, 16)
BAND = 128
SPAN = 2048
C_CONV = D_MODEL
CONV_WIDTH = 31
HALO = 32
D_FF = 2816
FF_CHUNK = D_FF
N_IN = 3 * D_ATT + 2 * C_CONV + 2 * D_MODEL
RMS_EPS = 1e-6
LN_EPS = 1e-5
NEG_INF = -1e30
ATT_SCALE = HEAD_DIM ** -0.5
Q_SCALE = ATT_SCALE * math.log2(math.e)
LANES = 128
LANE_TILES = C_CONV // LANES
VMEM_LIMIT = 56 * 1024 * 1024

O_Q, O_K, O_V = 0, D_ATT, 2 * D_ATT
O_UA = 3 * D_ATT
O_UB = O_UA + C_CONV
O_GA = O_UB + C_CONV
O_GB = O_GA + D_MODEL


def _dot(a, b):
    return jnp.dot(a, b, preferred_element_type=F32)


def _sigmoid(x):
    return 1.0 / (1.0 + jnp.exp(-x))


def _rms_rows(x, w):
    ms = jnp.mean(x * x, axis=-1, keepdims=True)
    return x * lax.rsqrt(ms + RMS_EPS) * w


def _head_norm_tiles(z, w_ref, scale):
    rows = z.shape[0]
    lane = lax.broadcasted_iota(jnp.int32, (rows, LANES), 1)
    low = lane < HEAD_DIM
    tiles = []
    for c in range(D_ATT // LANES):
        x = z[:, c * LANES:(c + 1) * LANES]
        x2 = x * x
        s_lo = jnp.sum(jnp.where(low, x2, 0.0), axis=-1, keepdims=True)
        s_hi = jnp.sum(jnp.where(low, 0.0, x2), axis=-1, keepdims=True)
        ms = jnp.where(low, s_lo, s_hi) * (1.0 / HEAD_DIM)
        y = x * lax.rsqrt(ms + RMS_EPS) * w_ref[:, c * LANES:(c + 1) * LANES]
        tiles.append(y * scale if scale != 1.0 else y)
    return tiles


def _in_projection(x, nw_ref, w_ref, qw_ref, kw_ref, store_qkv):
    xn = _rms_rows(x, nw_ref[...]).astype(BF16)
    for c, t in enumerate(_head_norm_tiles(_dot(xn, w_ref[:, O_Q:O_Q + D_ATT]), qw_ref, Q_SCALE)):
        store_qkv(0, c, t)
    for c, t in enumerate(_head_norm_tiles(_dot(xn, w_ref[:, O_K:O_K + D_ATT]), kw_ref, 1.0)):
        store_qkv(1, c, t)
    v = _dot(xn, w_ref[:, O_V:O_V + D_ATT])
    for c in range(D_ATT // LANES):
        store_qkv(2, c, v[:, c * LANES:(c + 1) * LANES])
    ua = _dot(xn, w_ref[:, O_UA:O_UA + C_CONV])
    ub = _dot(xn, w_ref[:, O_UB:O_UB + C_CONV])
    u = ua * _sigmoid(ub)
    sa = _sigmoid(_dot(xn, w_ref[:, O_GA:O_GA + D_MODEL]))
    sb = _sigmoid(_dot(xn, w_ref[:, O_GB:O_GB + D_MODEL]))
    return u, sa, sb


def _inproj_prompt_kernel(x_ref, nw_ref, w_ref, qw_ref, kw_ref,
                          q0, q1, q2, k0, k1, k2, v0, v1, v2, u_ref, sa_ref, sb_ref,
                          t0, t1, t2, ct_ref, qs_ref, ks_ref, vs_ref, *, tm, n_t):
    i = pl.program_id(1)
    stage = (qs_ref, ks_ref, vs_ref)

    def store_qkv(which, c, tile):
        stage[which][c] = tile

    u, sa, sb = _in_projection(x_ref[0], nw_ref, w_ref, qw_ref, kw_ref, store_qkv)
    u_ref[0] = u
    sa_ref[0] = sa.astype(sa_ref.dtype)
    sb_ref[0] = sb.astype(sb_ref.dtype)
    tiles_per_group = GROUP_W // LANES
    for src, outs in ((qs_ref, (q0, q1, q2)), (ks_ref, (k0, k1, k2)), (vs_ref, (v0, v1, v2))):
        for g, dil in enumerate(DILATIONS):
            for p in range(tiles_per_group):
                c = g * tiles_per_group + p
                lanes = slice(p * LANES, (p + 1) * LANES)
                if dil == 1:
                    outs[g][0, 0, :, lanes] = src[c].astype(BF16)
                else:
                    for r in range(dil):
                        outs[g][0, r, :, lanes] = src[c, pl.ds(r, tm // dil, stride=dil), :].astype(BF16)
    for g, (win, t_ref) in enumerate(zip(WINDOWS, (t0, t1, t2))):
        rows = min(win, tm)
        first = n_t - win // rows

        @pl.when(i >= first)
        def _(t_ref=t_ref, g=g, rows=rows):
            for which, src in enumerate((ks_ref, vs_ref)):
                for p in range(tiles_per_group):
                    c = g * tiles_per_group + p
                    t_ref[0, which, p * LANES:(p + 1) * LANES, :] = src[c, tm - rows:tm, :].T

    @pl.when(i == n_t - 1)
    def _():
        ct_ref[0] = u[tm - HALO:tm, :]


def _inproj_sample_kernel(x_ref, nw_ref, w_ref, qw_ref, kw_ref, q_ref, k_ref, v_ref, u_ref, sa_ref, sb_ref):
    outs = (q_ref, k_ref, v_ref)

    def store_qkv(which, c, tile):
        outs[which][:, c * LANES:(c + 1) * LANES] = tile

    u, sa, sb = _in_projection(x_ref[...], nw_ref, w_ref, qw_ref, kw_ref, store_qkv)
    for lt in range(LANE_TILES):
        u_ref[lt] = u[:, lt * LANES:(lt + 1) * LANES]
    sa_ref[...] = sa.astype(sa_ref.dtype)
    sb_ref[...] = sb.astype(sb_ref.dtype)


def _attn_unit(qs, kks, vvs, lo_limits):
    qi = lax.broadcasted_iota(jnp.int32, (2 * BAND, 2 * BAND), 0) & (BAND - 1)
    ki = lax.broadcasted_iota(jnp.int32, (2 * BAND, 2 * BAND), 1)
    band = (ki >= qi) & (ki <= qi + BAND)
    low = lax.broadcasted_iota(jnp.int32, (BAND, LANES), 1) < HEAD_DIM
    scores = []
    for q2, kk in zip(qs, kks):
        zero = jnp.zeros_like(q2)
        q_st = jnp.concatenate([jnp.where(low, q2, zero), jnp.where(low, zero, q2)], axis=0)
        scores.append(lax.dot_general(q_st, kk, (((1,), (1,)), ((), ())), preferred_element_type=F32))
    res = []
    for s, vv, lo in zip(scores, vvs, lo_limits):
        valid = band if isinstance(lo, int) and lo == 0 else band & (ki >= lo)
        s = jnp.where(valid, s, NEG_INF)
        m = jnp.max(s, axis=-1, keepdims=True)
        p = jnp.exp2(s - m)
        l = jnp.sum(p, axis=-1, keepdims=True)
        o = _dot(p.astype(BF16), vv) / l
        lse = m + jnp.log2(l)
        res.append((jnp.where(low, o[0:BAND], o[BAND:2 * BAND]),
                    jnp.where(low, lse[0:BAND], lse[BAND:2 * BAND])))
    return res


def _attn_prompt_kernel(q0, k0, v0, kp0, vp0, q1, k1, v1, kp1, vp1, q2, k2, v2, kp2, vp2,
                        o_ref, og_ref, lg_ref):
    sb = pl.program_id(1)
    refs = ((q0, k0, v0, kp0, vp0), (q1, k1, v1, kp1, vp1), (q2, k2, v2, kp2, vp2))
    first_lo = jnp.where(sb > 0, 0, BAND)
    tiles = GROUP_W // LANES

    for g in range(N_GROUPS):
        dil, win = DILATIONS[g], WINDOWS[g]
        q_ref, k_ref, v_ref, kp_ref, vp_ref = refs[g]

        def units(blocks, g=g, dil=dil, win=win, q_ref=q_ref, k_ref=k_ref, v_ref=v_ref,
                  kp_ref=kp_ref, vp_ref=vp_ref):
            qs, kks, vvs, los, dst = [], [], [], [], []
            for seg, r, first in blocks:
                row0 = seg * BAND
                if not isinstance(row0, int):
                    row0 = pl.multiple_of(row0, BAND)
                start = seg * win + r
                rows = pl.ds(start, BAND) if dil == 1 else pl.ds(start, BAND, stride=dil)
                for p in range(tiles):
                    lanes = slice(p * LANES, (p + 1) * LANES)
                    qs.append(q_ref[0, r, pl.ds(row0, BAND), lanes])
                    if first:
                        kks.append(jnp.concatenate([kp_ref[0, r, :, lanes], k_ref[0, r, 0:BAND, lanes]], axis=0))
                        vvs.append(jnp.concatenate([vp_ref[0, r, :, lanes], v_ref[0, r, 0:BAND, lanes]], axis=0))
                        los.append(first_lo)
                    else:
                        kks.append(k_ref[0, r, pl.ds(row0 - BAND, 2 * BAND), lanes])
                        vvs.append(v_ref[0, r, pl.ds(row0 - BAND, 2 * BAND), lanes])
                        los.append(0)
                    dst.append((g * tiles + p, rows))
            for (slot, rows), (o2, l2) in zip(dst, _attn_unit(qs, kks, vvs, los)):
                og_ref[slot, rows, :] = o2
                lg_ref[slot, rows, :] = l2

        shift = dil.bit_length() - 1
        per = 4
        if dil == 1:
            units([(0, 0, True), (1, 0, False), (2, 0, False), (3, 0, False)])

            def rest_body(t, c, units=units):
                units([(per + per * t + k, 0, False) for k in range(per)])
                return c
            lax.fori_loop(0, (SPAN // win - per) // per, rest_body, 0)
        else:
            def first_body(t, c, units=units):
                units([(0, per * t + k, True) for k in range(per)])
                return c
            lax.fori_loop(0, dil // per, first_body, 0)
            n_rest = (SPAN // win - 1) * dil
            if n_rest:
                def rest_body(t, c, units=units, dil=dil, shift=shift):
                    idx = [per * t + k for k in range(per)]
                    units([(1 + (a >> shift), a & (dil - 1), False) for a in idx])
                    return c
                lax.fori_loop(0, n_rest // per, rest_body, 0)

    chunk = 256

    def merge_body(c, carry):
        rows = pl.ds(pl.multiple_of(c * chunk, chunk), chunk)
        for p in range(tiles):
            l0, l1, l2 = lg_ref[p, rows, :], lg_ref[tiles + p, rows, :], lg_ref[2 * tiles + p, rows, :]
            m = jnp.maximum(jnp.maximum(l0, l1), l2)
            w0, w1, w2 = jnp.exp2(l0 - m), jnp.exp2(l1 - m), jnp.exp2(l2 - m)
            num = (w0 * og_ref[p, rows, :] + w1 * og_ref[tiles + p, rows, :]
                   + w2 * og_ref[2 * tiles + p, rows, :])
            o_ref[0, rows, p * LANES:(p + 1) * LANES] = (num / (w0 + w1 + w2)).astype(o_ref.dtype)
        return carry

    lax.fori_loop(0, SPAN // chunk, merge_body, 0)


def _merge_math(x, o_att, c, sa, sb, watt_ref, lnw_ref, lnb_ref, wco_ref, wout_ref):
    a = _dot(o_att.astype(BF16), watt_ref[...])
    mu = jnp.mean(c, axis=-1, keepdims=True)
    xc = c - mu
    var = jnp.mean(xc * xc, axis=-1, keepdims=True)
    y = xc * lax.rsqrt(var + LN_EPS) * lnw_ref[...] + lnb_ref[...]
    act = y * _sigmoid(y)
    cb = _dot(act.astype(BF16), wco_ref[...])
    h = sa.astype(F32) * a + sb.astype(F32) * cb
    return x + _dot(h.astype(BF16), wout_ref[...])


def _ffn_math(x, nw_ref, wgu_ref, wd_ref):
    xn = _rms_rows(x, nw_ref[...]).astype(BF16)
    acc = x
    for c0 in range(0, D_FF, FF_CHUNK):
        g = _dot(xn, wgu_ref[:, c0:c0 + FF_CHUNK])
        up = _dot(xn, wgu_ref[:, D_FF + c0:D_FF + c0 + FF_CHUNK])
        act = (g * _sigmoid(g) * up).astype(BF16)
        acc = acc + _dot(act, wd_ref[c0:c0 + FF_CHUNK, :])
    return acc


def _merge_ffn_prompt_kernel(x_ref, oa_ref, u_ref, uh_ref, sa_ref, sb_ref, watt_ref, cw_ref, cb_ref,
                             lnw_ref, lnb_ref, wco_ref, wout_ref, nfw_ref, wgu_ref, wd_ref,
                             y_ref, ubuf_ref, cbuf_ref, *, tm, blocks_per_seq):
    n = pl.program_id(0)
    seq_start = lax.rem(n, blocks_per_seq) == 0
    for lt in range(LANE_TILES):
        lanes = slice(lt * LANES, (lt + 1) * LANES)
        ubuf_ref[lt, 0:HALO, :] = jnp.where(seq_start, 0.0, uh_ref[:, lanes])
        ubuf_ref[lt, HALO:HALO + tm, :] = u_ref[:, lanes]
    rc = 64
    off = HALO - (CONV_WIDTH - 1)
    for lt in range(LANE_TILES):
        lanes = slice(lt * LANES, (lt + 1) * LANES)
        for parity in range(2):
            for r0 in range(0, tm // 2, rc):
                acc = jnp.broadcast_to(cb_ref[:, lanes], (rc, LANES))
                for j in range(CONV_WIDTH):
                    start = 2 * r0 + parity + off + j
                    acc = acc + ubuf_ref[lt, pl.ds(start, rc, stride=2), :] * cw_ref[j:j + 1, lanes]
                cbuf_ref[lt, pl.ds(2 * r0 + parity, rc, stride=2), :] = acc
    c = jnp.concatenate([cbuf_ref[lt] for lt in range(LANE_TILES)], axis=1)
    x1 = _merge_math(x_ref[...], oa_ref[...], c, sa_ref[...], sb_ref[...],
                     watt_ref, lnw_ref, lnb_ref, wco_ref, wout_ref)
    y_ref[...] = _ffn_math(x1, nfw_ref, wgu_ref, wd_ref)


def _merge_ffn_sample_kernel(x_ref, oa_ref, ut_ref, st_ref, sa_ref, sb_ref, cw_ref, cb_ref, watt_ref, lnw_ref,
                             lnb_ref, wco_ref, wout_ref, nfw_ref, wgu_ref, wd_ref, y_ref, cs_ref, c_scr,
                             *, n_new, n_seq):
    n_state = CONV_WIDTH - 1

    def ext_rows(k, lt):
        if k < n_state:
            return st_ref[k, :, lt * LANES:(lt + 1) * LANES]
        return ut_ref[lt, pl.ds(k - n_state, n_seq, stride=n_new), :]

    for lt in range(LANE_TILES):
        lanes = slice(lt * LANES, (lt + 1) * LANES)
        for s in range(n_new):
            acc = jnp.broadcast_to(cb_ref[:, lanes], (n_seq, LANES))
            for j in range(CONV_WIDTH):
                acc = acc + ext_rows(s + j, lt) * cw_ref[j:j + 1, lanes]
            c_scr[lt, pl.ds(s, n_seq, stride=n_new), :] = acc
        for k in range(n_state):
            cs_ref[k, :, lanes] = ext_rows(k + n_new, lt)
    c = jnp.concatenate([c_scr[lt] for lt in range(LANE_TILES)], axis=1)
    x1 = _merge_math(x_ref[...], oa_ref[...], c, sa_ref[...], sb_ref[...],
                     watt_ref, lnw_ref, lnb_ref, wco_ref, wout_ref)
    y_ref[...] = _ffn_math(x1, nfw_ref, wgu_ref, wd_ref)


def _shift_in(x, tail, n_new):
    rows, width = x.shape
    lane = lax.broadcasted_iota(jnp.int32, (rows, LANES), 1)
    keep = lane < LANES - n_new
    rolled = [pltpu.roll(x[:, t * LANES:(t + 1) * LANES], LANES - n_new, axis=1) for t in range(width // LANES)]
    rolled.append(tail)
    return jnp.concatenate([jnp.where(keep, rolled[t], rolled[t + 1]) for t in range(width // LANES)], axis=1)


def _sample_step_kernel(q_ref, kn_ref, vn_ref, c0_ref, c1_ref, c2_ref,
                        o_ref, s0_ref, s1_ref, s2_ref, new8_ref, tail_ref, *, n_new):
    caches = (c0_ref, c1_ref, c2_ref)
    outs = (s0_ref, s1_ref, s2_ref)
    pad = 8
    row = lax.broadcasted_iota(jnp.int32, (HEADS_PER_GROUP * pad, GROUP_W), 0)
    col = lax.broadcasted_iota(jnp.int32, (HEADS_PER_GROUP * pad, GROUP_W), 1)
    head_lanes = (col >> 6) == (row >> 3)
    new8_ref[...] = jnp.zeros(new8_ref.shape, F32)
    new8_ref[0, 0:n_new, :] = q_ref[0]
    new8_ref[1, 0:n_new, :] = kn_ref[0]
    new8_ref[2, 0:n_new, :] = vn_ref[0]
    tail_ref[...] = jnp.zeros(tail_ref.shape, F32)
    o_parts, l_parts = [], []
    for g, (win, dil) in enumerate(zip(WINDOWS, DILATIONS)):
        cols = slice(g * GROUP_W, (g + 1) * GROUP_W)
        kt = caches[g][0, 0]
        vt = caches[g][0, 1]
        for which, xt in enumerate((kt, vt)):
            tail_ref[:, 0:pad] = new8_ref[1 + which, :, cols].T
            tail = pltpu.roll(tail_ref[...], LANES - n_new, axis=1)
            outs[g][0, which] = _shift_in(xt, tail, n_new)
        q8 = new8_ref[0, :, cols]
        qm = jnp.where(head_lanes, jnp.concatenate([q8] * HEADS_PER_GROUP, axis=0), 0.0).astype(BF16)
        kn8 = new8_ref[1, :, cols].astype(BF16)
        vn8 = new8_ref[2, :, cols].astype(BF16)
        nt = (((1,), (1,)), ((), ()))
        s_c = _dot(qm, kt.astype(BF16))
        s_n = lax.dot_general(qm, kn8, nt, preferred_element_type=F32)
        sq_c = lax.broadcasted_iota(jnp.int32, s_c.shape, 0) & (pad - 1)
        kc_i = lax.broadcasted_iota(jnp.int32, s_c.shape, 1)
        d_c = win + sq_c - kc_i
        ok_c = ((d_c & (dil - 1)) == 0) & (d_c <= win)
        sq_n = lax.broadcasted_iota(jnp.int32, s_n.shape, 0) & (pad - 1)
        kn_i = lax.broadcasted_iota(jnp.int32, s_n.shape, 1)
        d_n = sq_n - kn_i
        ok_n = (d_n >= 0) & ((d_n & (dil - 1)) == 0) & (kn_i < n_new)
        s_c = jnp.where(ok_c, s_c, NEG_INF)
        s_n = jnp.where(ok_n, s_n, NEG_INF)
        m = jnp.maximum(jnp.max(s_c, axis=-1, keepdims=True), jnp.max(s_n, axis=-1, keepdims=True))
        p_c = jnp.exp2(s_c - m)
        p_n = jnp.exp2(s_n - m)
        l = jnp.sum(p_c, axis=-1, keepdims=True) + jnp.sum(p_n, axis=-1, keepdims=True)
        acc = (lax.dot_general(p_c.astype(BF16), vt.astype(BF16), nt, preferred_element_type=F32)
               + _dot(p_n.astype(BF16), vn8))
        o_parts.append(acc / l)
        l_parts.append(m + jnp.log2(l))
    m = jnp.maximum(jnp.maximum(l_parts[0], l_parts[1]), l_parts[2])
    w = [jnp.exp2(lp - m) for lp in l_parts]
    num = w[0] * o_parts[0] + w[1] * o_parts[1] + w[2] * o_parts[2]
    om = jnp.where(head_lanes, num / (w[0] + w[1] + w[2]), 0.0)
    o8 = om[0:pad] + om[pad:2 * pad] + om[2 * pad:3 * pad] + om[3 * pad:4 * pad]
    o_ref[0] = o8[0:n_new, :]


def _const_spec(shape, single=False):
    nd = len(shape)
    if single:
        return pl.BlockSpec(shape, lambda *_: (0,) * nd, pipeline_mode=pl.Buffered(1))
    return pl.BlockSpec(shape, lambda *_: (0,) * nd)


def _params(*sem):
    return pltpu.CompilerParams(dimension_semantics=sem, vmem_limit_bytes=VMEM_LIMIT)


def kernel(x_prompt, x_sample, cache_kv_w128, cache_kv_w512, cache_kv_w2048, state_conv, norm_mix_w, w_in,
           q_norm_w, k_norm_w, w_att, conv_w, conv_b, conv_ln_w, conv_ln_b, w_conv_out, w_out, norm_ffn_w,
           w_gate_up, w_down):
    B, T, D = x_prompt.shape
    SB, SS, _ = x_sample.shape
    assert D == D_MODEL and T % SPAN == 0 and norm_mix_w.shape[0] == 1
    caches = (cache_kv_w128, cache_kv_w512, cache_kv_w2048)
    for c, win in zip(caches, WINDOWS):
        assert c.shape[2] == win, "cached window shorter than the attention window is not supported"

    nmw = norm_mix_w.reshape(1, D)
    w_in_b = w_in[0].astype(BF16)
    qw = jnp.tile(q_norm_w[0], D_ATT // HEAD_DIM).reshape(1, D_ATT)
    kw = jnp.tile(k_norm_w[0], D_ATT // HEAD_DIM).reshape(1, D_ATT)
    w_att_b = w_att[0].astype(BF16)
    cw = conv_w[0]
    cb = conv_b.reshape(1, C_CONV)
    lnw = conv_ln_w.reshape(1, C_CONV)
    lnb = conv_ln_b.reshape(1, C_CONV)
    wco_b = w_conv_out[0].astype(BF16)
    wout_b = w_out[0].astype(BF16)
    nfw = norm_ffn_w.reshape(1, D)
    wgu_b = w_gate_up[0].astype(BF16)
    wd_b = w_down[0].astype(BF16)

    tm = 256
    n_t = T // tm
    perm_shapes = [jax.ShapeDtypeStruct((B, dil, T // dil, GROUP_W), BF16) for dil in DILATIONS]
    perm_specs = [pl.BlockSpec((1, dil, tm // dil, GROUP_W), lambda b, i: (b, 0, i, 0)) for dil in DILATIONS]
    tail_rows = [min(win, tm) for win in WINDOWS]
    tail_shapes = [jax.ShapeDtypeStruct((B, 2, GROUP_W, win), F32) for win in WINDOWS]
    tail_specs = [
        pl.BlockSpec((1, 2, GROUP_W, rows),
                     functools.partial(lambda b, i, first: (b, 0, 0, jnp.maximum(i - first, 0)),
                                       first=n_t - win // rows))
        for win, rows in zip(WINDOWS, tail_rows)]
    tok_spec = pl.BlockSpec((1, tm, D), lambda b, i: (b, i, 0))
    inproj_w_specs = [_const_spec((1, D)), _const_spec((D, N_IN), single=True), _const_spec((1, D_ATT)),
                      _const_spec((1, D_ATT))]
    res = pl.pallas_call(
        functools.partial(_inproj_prompt_kernel, tm=tm, n_t=n_t),
        grid=(B, n_t),
        in_specs=[tok_spec] + inproj_w_specs,
        out_specs=perm_specs * 3 + [tok_spec] * 3 + tail_specs
        + [pl.BlockSpec((1, HALO, C_CONV), lambda b, i: (b, 0, 0))],
        out_shape=perm_shapes * 3 + [jax.ShapeDtypeStruct((B, T, C_CONV), F32)]
        + [jax.ShapeDtypeStruct((B, T, D), BF16)] * 2 + tail_shapes
        + [jax.ShapeDtypeStruct((B, HALO, C_CONV), F32)],
        scratch_shapes=[pltpu.VMEM((D_ATT // LANES, tm, LANES), F32)] * 3,
        compiler_params=_params("arbitrary", "arbitrary"),
        name="inproj_prompt",
    )(x_prompt, nmw, w_in_b, qw, kw)
    qp, kp, vp = res[0:3], res[3:6], res[6:9]
    u_p, sa_p, sb_p = res[9:12]
    tails_p = res[12:15]
    conv_tail_p = res[15]

    n_sb = T // SPAN
    att_in, att_specs = [], []
    for g, dil in enumerate(DILATIONS):
        rows = SPAN // dil
        cur = pl.BlockSpec((1, dil, rows, GROUP_W), lambda b, s: (b, 0, s, 0))
        prev = pl.BlockSpec((1, dil, BAND, GROUP_W),
                            functools.partial(lambda b, s, n: (b, 0, jnp.maximum(s * n - 1, 0), 0),
                                              n=rows // BAND))
        att_in += [qp[g], kp[g], vp[g], kp[g], vp[g]]
        att_specs += [cur, cur, cur, prev, prev]
    o_att_p = pl.pallas_call(
        _attn_prompt_kernel,
        grid=(B, n_sb),
        in_specs=att_specs,
        out_specs=pl.BlockSpec((1, SPAN, GROUP_W), lambda b, s: (b, s, 0)),
        out_shape=jax.ShapeDtypeStruct((B, T, GROUP_W), BF16),
        scratch_shapes=[pltpu.VMEM((N_GROUPS * GROUP_W // LANES, SPAN, LANES), F32)] * 2,
        compiler_params=_params("arbitrary", "arbitrary"),
        name="attn_prompt",
    )(*att_in)

    tm3 = 512
    n_tok = B * T
    tok3 = lambda width: pl.BlockSpec((tm3, width), lambda n: (n, 0))
    merge_w_specs = [_const_spec((GROUP_W, D), single=True), _const_spec((1, C_CONV)), _const_spec((1, C_CONV)),
                     _const_spec((C_CONV, D), single=True), _const_spec((D, D), single=True)]
    conv_w_specs = [_const_spec((CONV_WIDTH, C_CONV)), _const_spec((1, C_CONV))]
    ffn_w_specs = [_const_spec((1, D)), _const_spec((D, 2 * D_FF), single=True),
                   _const_spec((D_FF, D), single=True)]
    u_flat = u_p.reshape(n_tok, C_CONV)
    y_p = pl.pallas_call(
        functools.partial(_merge_ffn_prompt_kernel, tm=tm3, blocks_per_seq=T // tm3),
        grid=(n_tok // tm3,),
        in_specs=[tok3(D), tok3(GROUP_W), tok3(C_CONV),
                  pl.BlockSpec((HALO, C_CONV), lambda n: (jnp.maximum(n * (tm3 // HALO) - 1, 0), 0)),
                  tok3(D), tok3(D), merge_w_specs[0]] + conv_w_specs + merge_w_specs[1:] + ffn_w_specs,
        out_specs=tok3(D),
        out_shape=jax.ShapeDtypeStruct((n_tok, D), F32),
        scratch_shapes=[pltpu.VMEM((LANE_TILES, tm3 + HALO, LANES), F32), pltpu.VMEM((LANE_TILES, tm3, LANES), F32)],
        compiler_params=_params("arbitrary"),
        name="merge_ffn_prompt",
    )(x_prompt.reshape(n_tok, D), o_att_p.reshape(n_tok, GROUP_W), u_flat, u_flat, sa_p.reshape(n_tok, D),
      sb_p.reshape(n_tok, D), w_att_b, cw, cb, lnw, lnb, wco_b, wout_b, nfw, wgu_b, wd_b).reshape(B, T, D)

    n_s = SB * SS
    n_state = CONV_WIDTH - 1
    xs2 = x_sample.reshape(n_s, D)
    row_shapes = ([jax.ShapeDtypeStruct((n_s, D_ATT), F32)] * 3
                  + [jax.ShapeDtypeStruct((LANE_TILES, n_s, LANES), F32)]
                  + [jax.ShapeDtypeStruct((n_s, D), BF16)] * 2)
    q_s, k_s, v_s, ut_s, sa_s, sb_s = pl.pallas_call(
        _inproj_sample_kernel,
        grid=(1,),
        in_specs=[_const_spec((n_s, D))] + inproj_w_specs,
        out_specs=[_const_spec(s.shape) for s in row_shapes],
        out_shape=row_shapes,
        compiler_params=_params("arbitrary"),
        name="inproj_sample",
    )(xs2, nmw, w_in_b, qw, kw)

    seq_spec = lambda *dims: pl.BlockSpec((1,) + dims, lambda b: (b,) + (0,) * len(dims))
    cache_t = [jnp.transpose(c[0], (0, 2, 3, 4, 1)).reshape(SB, 2, GROUP_W, win) for c, win in zip(caches, WINDOWS)]
    win_shapes = [jax.ShapeDtypeStruct((SB, 2, GROUP_W, win), F32) for win in WINDOWS]
    win_specs = [seq_spec(2, GROUP_W, win) for win in WINDOWS]
    o_att_s, kv0_s, kv1_s, kv2_s = pl.pallas_call(
        functools.partial(_sample_step_kernel, n_new=SS),
        grid=(SB,),
        in_specs=[seq_spec(SS, D_ATT)] * 3 + win_specs,
        out_specs=[seq_spec(SS, GROUP_W)] + win_specs,
        out_shape=[jax.ShapeDtypeStruct((SB, SS, GROUP_W), F32)] + win_shapes,
        scratch_shapes=[pltpu.VMEM((3, 8, D_ATT), F32), pltpu.VMEM((GROUP_W, LANES), F32)],
        compiler_params=_params("arbitrary"),
        name="sample_step",
    )(q_s.reshape(SB, SS, D_ATT), k_s.reshape(SB, SS, D_ATT), v_s.reshape(SB, SS, D_ATT), *cache_t)

    state_t = jnp.transpose(state_conv[0], (1, 0, 2))
    y_s, conv_state_t = pl.pallas_call(
        functools.partial(_merge_ffn_sample_kernel, n_new=SS, n_seq=SB),
        grid=(1,),
        in_specs=[_const_spec((n_s, D)), _const_spec((n_s, GROUP_W)), _const_spec((LANE_TILES, n_s, LANES)),
                  _const_spec((n_state, SB, C_CONV)), _const_spec((n_s, D)), _const_spec((n_s, D))]
        + conv_w_specs + merge_w_specs + ffn_w_specs,
        out_specs=[_const_spec((n_s, D)), _const_spec((n_state, SB, C_CONV))],
        out_shape=[jax.ShapeDtypeStruct((n_s, D), F32), jax.ShapeDtypeStruct((n_state, SB, C_CONV), F32)],
        scratch_shapes=[pltpu.VMEM((LANE_TILES, n_s, LANES), F32)],
        compiler_params=_params("arbitrary"),
        name="merge_ffn_sample",
    )(xs2, o_att_s.reshape(n_s, GROUP_W), ut_s, state_t, sa_s, sb_s, cw, cb, w_att_b, lnw, lnb, wco_b, wout_b,
      nfw, wgu_b, wd_b)

    def kv_cols(a, nb, win):
        return a.reshape(nb, 2, HEADS_PER_GROUP, HEAD_DIM, win).transpose(0, 4, 1, 2, 3)[None]

    return (y_p, y_s.reshape(SB, SS, D),
            kv_cols(tails_p[0], B, WINDOWS[0]), kv_cols(kv0_s, SB, WINDOWS[0]),
            kv_cols(tails_p[1], B, WINDOWS[1]), kv_cols(kv1_s, SB, WINDOWS[1]),
            kv_cols(tails_p[2], B, WINDOWS[2]), kv_cols(kv2_s, SB, WINDOWS[2]),
            conv_tail_p[:, HALO - n_state:, :][None], jnp.transpose(conv_state_t, (1, 0, 2))[None])
```

```python
import functools
import math

import jax
import jax.numpy as jnp
from jax import lax
from jax.experimental import pallas as pl
from jax.experimental.pallas import tpu as pltpu

F32 = jnp.float32
BF16 = jnp.bfloat16

D_MODEL = 1024
HEAD_DIM = 64
HEADS_PER_GROUP = 4
GROUP_W = HEADS_PER_GROUP * HEAD_DIM
N_GROUPS = 3
D_ATT = N_GROUPS * GROUP_W
WINDOWS = (128, 512, 2048)
DILATIONS = (1, 4, 16)
BAND = 128
SPAN = 2048
C_CONV = D_MODEL
CONV_WIDTH = 31
HALO = 32
D_FF = 2816
FF_CHUNK = D_FF
N_IN = 3 * D_ATT + 2 * C_CONV + 2 * D_MODEL
RMS_EPS = 1e-6
LN_EPS = 1e-5
NEG_INF = -1e30
ATT_SCALE = HEAD_DIM ** -0.5
Q_SCALE = ATT_SCALE * math.log2(math.e)
LANES = 128
LANE_TILES = C_CONV // LANES
VMEM_LIMIT = 56 * 1024 * 1024

O_Q, O_K, O_V = 0, D_ATT, 2 * D_ATT
O_UA = 3 * D_ATT
O_UB = O_UA + C_CONV
O_GA = O_UB + C_CONV
O_GB = O_GA + D_MODEL


def _dot(a, b):
    return jnp.dot(a, b, preferred_element_type=F32)


def _sigmoid(x):
    return 1.0 / (1.0 + jnp.exp(-x))


def _rms_rows(x, w):
    ms = jnp.mean(x * x, axis=-1, keepdims=True)
    return x * lax.rsqrt(ms + RMS_EPS) * w


def _head_norm_tiles(z, w_ref, scale):
    rows = z.shape[0]
    lane = lax.broadcasted_iota(jnp.int32, (rows, LANES), 1)
    low = lane < HEAD_DIM
    tiles = []
    for c in range(D_ATT // LANES):
        x = z[:, c * LANES:(c + 1) * LANES]
        x2 = x * x
        s_lo = jnp.sum(jnp.where(low, x2, 0.0), axis=-1, keepdims=True)
        s_hi = jnp.sum(jnp.where(low, 0.0, x2), axis=-1, keepdims=True)
        ms = jnp.where(low, s_lo, s_hi) * (1.0 / HEAD_DIM)
        y = x * lax.rsqrt(ms + RMS_EPS) * w_ref[:, c * LANES:(c + 1) * LANES]
        tiles.append(y * scale if scale != 1.0 else y)
    return tiles


def _in_projection(x, nw_ref, w_ref, qw_ref, kw_ref, store_qkv):
    xn = _rms_rows(x, nw_ref[...]).astype(BF16)
    for c, t in enumerate(_head_norm_tiles(_dot(xn, w_ref[:, O_Q:O_Q + D_ATT]), qw_ref, Q_SCALE)):
        store_qkv(0, c, t)
    for c, t in enumerate(_head_norm_tiles(_dot(xn, w_ref[:, O_K:O_K + D_ATT]), kw_ref, 1.0)):
        store_qkv(1, c, t)
    v = _dot(xn, w_ref[:, O_V:O_V + D_ATT])
    for c in range(D_ATT // LANES):
        store_qkv(2, c, v[:, c * LANES:(c + 1) * LANES])
    ua = _dot(xn, w_ref[:, O_UA:O_UA + C_CONV])
    ub = _dot(xn, w_ref[:, O_UB:O_UB + C_CONV])
    u = ua * _sigmoid(ub)
    sa = _sigmoid(_dot(xn, w_ref[:, O_GA:O_GA + D_MODEL]))
    sb = _sigmoid(_dot(xn, w_ref[:, O_GB:O_GB + D_MODEL]))
    return u, sa, sb


def _inproj_prompt_kernel(x_ref, nw_ref, w_ref, qw_ref, kw_ref,
                          qkv0, qkv1, qkv2, u_ref, gates_ref,
                          t0, t1, t2, ct_ref, qs_ref, ks_ref, vs_ref, *, tm, n_t):
    i = pl.program_id(1)
    stage = (qs_ref, ks_ref, vs_ref)

    def store_qkv(which, c, tile):
        stage[which][c] = tile

    u, sa, sb = _in_projection(x_ref[0], nw_ref, w_ref, qw_ref, kw_ref, store_qkv)
    u_ref[0] = u
    gates_ref[0, :, 0:D_MODEL] = sa.astype(gates_ref.dtype)
    gates_ref[0, :, D_MODEL:2 * D_MODEL] = sb.astype(gates_ref.dtype)
    tiles_per_group = GROUP_W // LANES
    for which, src in enumerate((qs_ref, ks_ref, vs_ref)):
        for g, (dil, out) in enumerate(zip(DILATIONS, (qkv0, qkv1, qkv2))):
            for p in range(tiles_per_group):
                c = g * tiles_per_group + p
                lanes = slice(which * GROUP_W + p * LANES, which * GROUP_W + (p + 1) * LANES)
                if dil == 1:
                    out[0, 0, :, lanes] = src[c].astype(BF16)
                else:
                    for r in range(dil):
                        out[0, r, :, lanes] = src[c, pl.ds(r, tm // dil, stride=dil), :].astype(BF16)
    for g, (win, t_ref) in enumerate(zip(WINDOWS, (t0, t1, t2))):
        rows = min(win, tm)
        first = n_t - win // rows

        @pl.when(i >= first)
        def _(t_ref=t_ref, g=g, rows=rows):
            for which, src in enumerate((ks_ref, vs_ref)):
                for p in range(tiles_per_group):
                    c = g * tiles_per_group + p
                    t_ref[0, which, p * LANES:(p + 1) * LANES, :] = src[c, tm - rows:tm, :].T

    @pl.when(i == n_t - 1)
    def _():
        ct_ref[0] = u[tm - HALO:tm, :]


def _inproj_sample_kernel(x_ref, nw_ref, w_ref, qw_ref, kw_ref, q_ref, k_ref, v_ref, u_ref, sa_ref, sb_ref):
    outs = (q_ref, k_ref, v_ref)

    def store_qkv(which, c, tile):
        outs[which][:, c * LANES:(c + 1) * LANES] = tile

    u, sa, sb = _in_projection(x_ref[...], nw_ref, w_ref, qw_ref, kw_ref, store_qkv)
    for lt in range(LANE_TILES):
        u_ref[lt] = u[:, lt * LANES:(lt + 1) * LANES]
    sa_ref[...] = sa.astype(sa_ref.dtype)
    sb_ref[...] = sb.astype(sb_ref.dtype)


def _attn_unit(qs, kks, vvs, lo_limits):
    qi = lax.broadcasted_iota(jnp.int32, (2 * BAND, 2 * BAND), 0) & (BAND - 1)
    ki = lax.broadcasted_iota(jnp.int32, (2 * BAND, 2 * BAND), 1)
    band = (ki >= qi) & (ki <= qi + BAND)
    low = lax.broadcasted_iota(jnp.int32, (BAND, LANES), 1) < HEAD_DIM
    scores = []
    for q2, kk in zip(qs, kks):
        zero = jnp.zeros_like(q2)
        q_st = jnp.concatenate([jnp.where(low, q2, zero), jnp.where(low, zero, q2)], axis=0)
        scores.append(lax.dot_general(q_st, kk, (((1,), (1,)), ((), ())), preferred_element_type=F32))
    res = []
    for s, vv, lo in zip(scores, vvs, lo_limits):
        valid = band if isinstance(lo, int) and lo == 0 else band & (ki >= lo)
        s = jnp.where(valid, s, NEG_INF)
        m = jnp.max(s, axis=-1, keepdims=True)
        p = jnp.exp2(s - m)
        l = jnp.sum(p, axis=-1, keepdims=True)
        o = _dot(p.astype(BF16), vv) / l
        lse = m + jnp.log2(l)
        res.append((jnp.where(low, o[0:BAND], o[BAND:2 * BAND]),
                    jnp.where(low, lse[0:BAND], lse[BAND:2 * BAND])))
    return res


def _attn_prompt_kernel(cur0, prev0, cur1, prev1, cur2, prev2, o_ref, og_ref, lg_ref):
    sb = pl.program_id(1)
    refs = ((cur0, prev0), (cur1, prev1), (cur2, prev2))
    first_lo = jnp.where(sb > 0, 0, BAND)
    tiles = GROUP_W // LANES

    for g in range(N_GROUPS):
        dil, win = DILATIONS[g], WINDOWS[g]
        cur_ref, prev_ref = refs[g]

        def units(blocks, g=g, dil=dil, win=win, cur_ref=cur_ref, prev_ref=prev_ref):
            qs, kks, vvs, los, dst = [], [], [], [], []
            for seg, r, first in blocks:
                row0 = seg * BAND
                if not isinstance(row0, int):
                    row0 = pl.multiple_of(row0, BAND)
                start = seg * win + r
                rows = pl.ds(start, BAND) if dil == 1 else pl.ds(start, BAND, stride=dil)
                for p in range(tiles):
                    ql = slice(p * LANES, (p + 1) * LANES)
                    kl = slice(GROUP_W + p * LANES, GROUP_W + (p + 1) * LANES)
                    vl = slice(2 * GROUP_W + p * LANES, 2 * GROUP_W + (p + 1) * LANES)
                    qs.append(cur_ref[0, r, pl.ds(row0, BAND), ql])
                    if first:
                        kks.append(jnp.concatenate([prev_ref[0, r, :, kl], cur_ref[0, r, 0:BAND, kl]], axis=0))
                        vvs.append(jnp.concatenate([prev_ref[0, r, :, vl], cur_ref[0, r, 0:BAND, vl]], axis=0))
                        los.append(first_lo)
                    else:
                        kks.append(cur_ref[0, r, pl.ds(row0 - BAND, 2 * BAND), kl])
                        vvs.append(cur_ref[0, r, pl.ds(row0 - BAND, 2 * BAND), vl])
                        los.append(0)
                    dst.append((g * tiles + p, rows))
            for (slot, rows), (o2, l2) in zip(dst, _attn_unit(qs, kks, vvs, los)):
                og_ref[slot, rows, :] = o2
                lg_ref[slot, rows, :] = l2

        shift = dil.bit_length() - 1
        per = 4
        if dil == 1:
            units([(0, 0, True), (1, 0, False), (2, 0, False), (3, 0, False)])

            def rest_body(t, c, units=units):
                units([(per + per * t + k, 0, False) for k in range(per)])
                return c
            lax.fori_loop(0, (SPAN // win - per) // per, rest_body, 0)
        else:
            def first_body(t, c, units=units):
                units([(0, per * t + k, True) for k in range(per)])
                return c
            lax.fori_loop(0, dil // per, first_body, 0)
            n_rest = (SPAN // win - 1) * dil
            if n_rest:
                def rest_body(t, c, units=units, dil=dil, shift=shift):
                    idx = [per * t + k for k in range(per)]
                    units([(1 + (a >> shift), a & (dil - 1), False) for a in idx])
                    return c
                lax.fori_loop(0, n_rest // per, rest_body, 0)

    chunk = 256

    def merge_body(c, carry):
        rows = pl.ds(pl.multiple_of(c * chunk, chunk), chunk)
        for p in range(tiles):
            l0, l1, l2 = lg_ref[p, rows, :], lg_ref[tiles + p, rows, :], lg_ref[2 * tiles + p, rows, :]
            m = jnp.maximum(jnp.maximum(l0, l1), l2)
            w0, w1, w2 = jnp.exp2(l0 - m), jnp.exp2(l1 - m), jnp.exp2(l2 - m)
            num = (w0 * og_ref[p, rows, :] + w1 * og_ref[tiles + p, rows, :]
                   + w2 * og_ref[2 * tiles + p, rows, :])
            o_ref[0, rows, p * LANES:(p + 1) * LANES] = (num / (w0 + w1 + w2)).astype(o_ref.dtype)
        return carry

    lax.fori_loop(0, SPAN // chunk, merge_body, 0)


def _merge_math(x, o_att, c, sa, sb, watt_ref, lnw_ref, lnb_ref, wco_ref, wout_ref):
    a = _dot(o_att.astype(BF16), watt_ref[...])
    mu = jnp.mean(c, axis=-1, keepdims=True)
    xc = c - mu
    var = jnp.mean(xc * xc, axis=-1, keepdims=True)
    y = xc * lax.rsqrt(var + LN_EPS) * lnw_ref[...] + lnb_ref[...]
    act = y * _sigmoid(y)
    cb = _dot(act.astype(BF16), wco_ref[...])
    h = sa.astype(F32) * a + sb.astype(F32) * cb
    return x + _dot(h.astype(BF16), wout_ref[...])


def _ffn_math(x, nw_ref, wgu_ref, wd_ref):
    xn = _rms_rows(x, nw_ref[...]).astype(BF16)
    acc = x
    for c0 in range(0, D_FF, FF_CHUNK):
        g = _dot(xn, wgu_ref[:, c0:c0 + FF_CHUNK])
        up = _dot(xn, wgu_ref[:, D_FF + c0:D_FF + c0 + FF_CHUNK])
        act = (g * _sigmoid(g) * up).astype(BF16)
        acc = acc + _dot(act, wd_ref[c0:c0 + FF_CHUNK, :])
    return acc


def _merge_ffn_prompt_kernel(x_ref, oa_ref, u_ref, uh_ref, gates_ref, watt_ref, cw_ref, cb_ref,
                             lnw_ref, lnb_ref, wco_ref, wout_ref, nfw_ref, wgu_ref, wd_ref,
                             y_ref, ubuf_ref, cbuf_ref, *, tm, blocks_per_seq):
    n = pl.program_id(0)
    seq_start = lax.rem(n, blocks_per_seq) == 0
    for lt in range(LANE_TILES):
        lanes = slice(lt * LANES, (lt + 1) * LANES)
        ubuf_ref[lt, 0:HALO, :] = jnp.where(seq_start, 0.0, uh_ref[:, lanes])
        ubuf_ref[lt, HALO:HALO + tm, :] = u_ref[:, lanes]
    rc = 64
    off = HALO - (CONV_WIDTH - 1)
    for lt in range(LANE_TILES):
        lanes = slice(lt * LANES, (lt + 1) * LANES)
        for parity in range(2):
            for r0 in range(0, tm // 2, rc):
                acc = jnp.broadcast_to(cb_ref[:, lanes], (rc, LANES))
                for j in range(CONV_WIDTH):
                    start = 2 * r0 + parity + off + j
                    acc = acc + ubuf_ref[lt, pl.ds(start, rc, stride=2), :] * cw_ref[j:j + 1, lanes]
                cbuf_ref[lt, pl.ds(2 * r0 + parity, rc, stride=2), :] = acc
    c = jnp.concatenate([cbuf_ref[lt] for lt in range(LANE_TILES)], axis=1)
    x1 = _merge_math(x_ref[...], oa_ref[...], c, gates_ref[:, 0:D_MODEL], gates_ref[:, D_MODEL:2 * D_MODEL],
                     watt_ref, lnw_ref, lnb_ref, wco_ref, wout_ref)
    y_ref[...] = _ffn_math(x1, nfw_ref, wgu_ref, wd_ref)


def _merge_ffn_sample_kernel(x_ref, oa_ref, ut_ref, st_ref, sa_ref, sb_ref, cw_ref, cb_ref, watt_ref, lnw_ref,
                             lnb_ref, wco_ref, wout_ref, nfw_ref, wgu_ref, wd_ref, y_ref, cs_ref, c_scr,
                             *, n_new, n_seq):
    n_state = CONV_WIDTH - 1

    def ext_rows(k, lt):
        if k < n_state:
            return st_ref[k, :, lt * LANES:(lt + 1) * LANES]
        return ut_ref[lt, pl.ds(k - n_state, n_seq, stride=n_new), :]

    for lt in range(LANE_TILES):
        lanes = slice(lt * LANES, (lt + 1) * LANES)
        for s in range(n_new):
            acc = jnp.broadcast_to(cb_ref[:, lanes], (n_seq, LANES))
            for j in range(CONV_WIDTH):
                acc = acc + ext_rows(s + j, lt) * cw_ref[j:j + 1, lanes]
            c_scr[lt, pl.ds(s, n_seq, stride=n_new), :] = acc
        for k in range(n_state):
            cs_ref[k, :, lanes] = ext_rows(k + n_new, lt)
    c = jnp.concatenate([c_scr[lt] for lt in range(LANE_TILES)], axis=1)
    x1 = _merge_math(x_ref[...], oa_ref[...], c, sa_ref[...], sb_ref[...],
                     watt_ref, lnw_ref, lnb_ref, wco_ref, wout_ref)
    y_ref[...] = _ffn_math(x1, nfw_ref, wgu_ref, wd_ref)


def _shift_in(x, tail, n_new):
    rows, width = x.shape
    lane = lax.broadcasted_iota(jnp.int32, (rows, LANES), 1)
    keep = lane < LANES - n_new
    rolled = [pltpu.roll(x[:, t * LANES:(t + 1) * LANES], LANES - n_new, axis=1) for t in range(width // LANES)]
    rolled.append(tail)
    return jnp.concatenate([jnp.where(keep, rolled[t], rolled[t + 1]) for t in range(width // LANES)], axis=1)


def _sample_step_kernel(q_ref, kn_ref, vn_ref, c0_ref, c1_ref, c2_ref,
                        o_ref, s0_ref, s1_ref, s2_ref, new8_ref, tail_ref, *, n_new):
    caches = (c0_ref, c1_ref, c2_ref)
    outs = (s0_ref, s1_ref, s2_ref)
    pad = 8
    row = lax.broadcasted_iota(jnp.int32, (HEADS_PER_GROUP * pad, GROUP_W), 0)
    col = lax.broadcasted_iota(jnp.int32, (HEADS_PER_GROUP * pad, GROUP_W), 1)
    head_lanes = (col >> 6) == (row >> 3)
    new8_ref[...] = jnp.zeros(new8_ref.shape, F32)
    new8_ref[0, 0:n_new, :] = q_ref[0]
    new8_ref[1, 0:n_new, :] = kn_ref[0]
    new8_ref[2, 0:n_new, :] = vn_ref[0]
    tail_ref[...] = jnp.zeros(tail_ref.shape, F32)
    o_parts, l_parts = [], []
    for g, (win, dil) in enumerate(zip(WINDOWS, DILATIONS)):
        cols = slice(g * GROUP_W, (g + 1) * GROUP_W)
        kt = caches[g][0, 0]
        vt = caches[g][0, 1]
        for which, xt in enumerate((kt, vt)):
            tail_ref[:, 0:pad] = new8_ref[1 + which, :, cols].T
            tail = pltpu.roll(tail_ref[...], LANES - n_new, axis=1)
            outs[g][0, which] = _shift_in(xt, tail, n_new)
        q8 = new8_ref[0, :, cols]
        qm = jnp.where(head_lanes, jnp.concatenate([q8] * HEADS_PER_GROUP, axis=0), 0.0).astype(BF16)
        kn8 = new8_ref[1, :, cols].astype(BF16)
        vn8 = new8_ref[2, :, cols].astype(BF16)
        nt = (((1,), (1,)), ((), ()))
        s_c = _dot(qm, kt.astype(BF16))
        s_n = lax.dot_general(qm, kn8, nt, preferred_element_type=F32)
        sq_c = lax.broadcasted_iota(jnp.int32, s_c.shape, 0) & (pad - 1)
        kc_i = lax.broadcasted_iota(jnp.int32, s_c.shape, 1)
        d_c = win + sq_c - kc_i
        ok_c = ((d_c & (dil - 1)) == 0) & (d_c <= win)
        sq_n = lax.broadcasted_iota(jnp.int32, s_n.shape, 0) & (pad - 1)
        kn_i = lax.broadcasted_iota(jnp.int32, s_n.shape, 1)
        d_n = sq_n - kn_i
        ok_n = (d_n >= 0) & ((d_n & (dil - 1)) == 0) & (kn_i < n_new)
        s_c = jnp.where(ok_c, s_c, NEG_INF)
        s_n = jnp.where(ok_n, s_n, NEG_INF)
        m = jnp.maximum(jnp.max(s_c, axis=-1, keepdims=True), jnp.max(s_n, axis=-1, keepdims=True))
        p_c = jnp.exp2(s_c - m)
        p_n = jnp.exp2(s_n - m)
        l = jnp.sum(p_c, axis=-1, keepdims=True) + jnp.sum(p_n, axis=-1, keepdims=True)
        acc = (lax.dot_general(p_c.astype(BF16), vt.astype(BF16), nt, preferred_element_type=F32)
               + _dot(p_n.astype(BF16), vn8))
        o_parts.append(acc / l)
        l_parts.append(m + jnp.log2(l))
    m = jnp.maximum(jnp.maximum(l_parts[0], l_parts[1]), l_parts[2])
    w = [jnp.exp2(lp - m) for lp in l_parts]
    num = w[0] * o_parts[0] + w[1] * o_parts[1] + w[2] * o_parts[2]
    om = jnp.where(head_lanes, num / (w[0] + w[1] + w[2]), 0.0)
    o8 = om[0:pad] + om[pad:2 * pad] + om[2 * pad:3 * pad] + om[3 * pad:4 * pad]
    o_ref[0] = o8[0:n_new, :]


def _const_spec(shape, single=False):
    nd = len(shape)
    if single:
        return pl.BlockSpec(shape, lambda *_: (0,) * nd, pipeline_mode=pl.Buffered(1))
    return pl.BlockSpec(shape, lambda *_: (0,) * nd)


def _params(*sem):
    return pltpu.CompilerParams(dimension_semantics=sem, vmem_limit_bytes=VMEM_LIMIT)


def kernel(x_prompt, x_sample, cache_kv_w128, cache_kv_w512, cache_kv_w2048, state_conv, norm_mix_w, w_in,
           q_norm_w, k_norm_w, w_att, conv_w, conv_b, conv_ln_w, conv_ln_b, w_conv_out, w_out, norm_ffn_w,
           w_gate_up, w_down):
    B, T, D = x_prompt.shape
    SB, SS, _ = x_sample.shape
    assert D == D_MODEL and T % SPAN == 0 and norm_mix_w.shape[0] == 1
    caches = (cache_kv_w128, cache_kv_w512, cache_kv_w2048)
    for c, win in zip(caches, WINDOWS):
        assert c.shape[2] == win, "cached window shorter than the attention window is not supported"

    nmw = norm_mix_w.reshape(1, D)
    w_in_b = w_in[0].astype(BF16)
    qw = jnp.tile(q_norm_w[0], D_ATT // HEAD_DIM).reshape(1, D_ATT)
    kw = jnp.tile(k_norm_w[0], D_ATT // HEAD_DIM).reshape(1, D_ATT)
    w_att_b = w_att[0].astype(BF16)
    cw = conv_w[0]
    cb = conv_b.reshape(1, C_CONV)
    lnw = conv_ln_w.reshape(1, C_CONV)
    lnb = conv_ln_b.reshape(1, C_CONV)
    wco_b = w_conv_out[0].astype(BF16)
    wout_b = w_out[0].astype(BF16)
    nfw = norm_ffn_w.reshape(1, D)
    wgu_b = w_gate_up[0].astype(BF16)
    wd_b = w_down[0].astype(BF16)

    tm = 256
    n_t = T // tm
    perm_shapes = [jax.ShapeDtypeStruct((B, dil, T // dil, D_ATT), BF16) for dil in DILATIONS]
    perm_specs = [pl.BlockSpec((1, dil, tm // dil, D_ATT), lambda b, i: (b, 0, i, 0)) for dil in DILATIONS]
    tail_rows = [min(win, tm) for win in WINDOWS]
    tail_shapes = [jax.ShapeDtypeStruct((B, 2, GROUP_W, win), F32) for win in WINDOWS]
    tail_specs = [
        pl.BlockSpec((1, 2, GROUP_W, rows),
                     functools.partial(lambda b, i, first: (b, 0, 0, jnp.maximum(i - first, 0)),
                                       first=n_t - win // rows))
        for win, rows in zip(WINDOWS, tail_rows)]
    tok_spec = pl.BlockSpec((1, tm, D), lambda b, i: (b, i, 0))
    inproj_w_specs = [_const_spec((1, D)), _const_spec((D, N_IN), single=True), _const_spec((1, D_ATT)),
                      _const_spec((1, D_ATT))]
    res = pl.pallas_call(
        functools.partial(_inproj_prompt_kernel, tm=tm, n_t=n_t),
        grid=(B, n_t),
        in_specs=[tok_spec] + inproj_w_specs,
        out_specs=perm_specs + [tok_spec, pl.BlockSpec((1, tm, 2 * D), lambda b, i: (b, i, 0))] + tail_specs
        + [pl.BlockSpec((1, HALO, C_CONV), lambda b, i: (b, 0, 0))],
        out_shape=perm_shapes + [jax.ShapeDtypeStruct((B, T, C_CONV), F32)]
        + [jax.ShapeDtypeStruct((B, T, 2 * D), BF16)] + tail_shapes
        + [jax.ShapeDtypeStruct((B, HALO, C_CONV), F32)],
        scratch_shapes=[pltpu.VMEM((D_ATT // LANES, tm, LANES), F32)] * 3,
        compiler_params=_params("arbitrary", "arbitrary"),
        name="inproj_prompt",
    )(x_prompt, nmw, w_in_b, qw, kw)
    qkv_p = res[0:3]
    u_p, gates_p = res[3:5]
    tails_p = res[5:8]
    conv_tail_p = res[8]

    n_sb = T // SPAN
    att_in, att_specs = [], []
    for g, dil in enumerate(DILATIONS):
        rows = SPAN // dil
        cur = pl.BlockSpec((1, dil, rows, D_ATT), lambda b, s: (b, 0, s, 0))
        prev = pl.BlockSpec((1, dil, BAND, D_ATT),
                            functools.partial(lambda b, s, n: (b, 0, jnp.maximum(s * n - 1, 0), 0),
                                              n=rows // BAND))
        att_in += [qkv_p[g], qkv_p[g]]
        att_specs += [cur, prev]
    o_att_p = pl.pallas_call(
        _attn_prompt_kernel,
        grid=(B, n_sb),
        in_specs=att_specs,
        out_specs=pl.BlockSpec((1, SPAN, GROUP_W), lambda b, s: (b, s, 0)),
        out_shape=jax.ShapeDtypeStruct((B, T, GROUP_W), BF16),
        scratch_shapes=[pltpu.VMEM((N_GROUPS * GROUP_W // LANES, SPAN, LANES), F32)] * 2,
        compiler_params=_params("arbitrary", "arbitrary"),
        name="attn_prompt",
    )(*att_in)

    tm3 = 512
    n_tok = B * T
    tok3 = lambda width: pl.BlockSpec((tm3, width), lambda n: (n, 0))
    merge_w_specs = [_const_spec((GROUP_W, D), single=True), _const_spec((1, C_CONV)), _const_spec((1, C_CONV)),
                     _const_spec((C_CONV, D), single=True), _const_spec((D, D), single=True)]
    conv_w_specs = [_const_spec((CONV_WIDTH, C_CONV)), _const_spec((1, C_CONV))]
    ffn_w_specs = [_const_spec((1, D)), _const_spec((D, 2 * D_FF), single=True),
                   _const_spec((D_FF, D), single=True)]
    u_flat = u_p.reshape(n_tok, C_CONV)
    y_p = pl.pallas_call(
        functools.partial(_merge_ffn_prompt_kernel, tm=tm3, blocks_per_seq=T // tm3),
        grid=(n_tok // tm3,),
        in_specs=[tok3(D), tok3(GROUP_W), tok3(C_CONV),
                  pl.BlockSpec((HALO, C_CONV), lambda n: (jnp.maximum(n * (tm3 // HALO) - 1, 0), 0)),
                  tok3(2 * D), merge_w_specs[0]] + conv_w_specs + merge_w_specs[1:] + ffn_w_specs,
        out_specs=tok3(D),
        out_shape=jax.ShapeDtypeStruct((n_tok, D), F32),
        scratch_shapes=[pltpu.VMEM((LANE_TILES, tm3 + HALO, LANES), F32), pltpu.VMEM((LANE_TILES, tm3, LANES), F32)],
        compiler_params=_params("arbitrary"),
        name="merge_ffn_prompt",
    )(x_prompt.reshape(n_tok, D), o_att_p.reshape(n_tok, GROUP_W), u_flat, u_flat, gates_p.reshape(n_tok, 2 * D),
      w_att_b, cw, cb, lnw, lnb, wco_b, wout_b, nfw, wgu_b, wd_b).reshape(B, T, D)

    n_s = SB * SS
    n_state = CONV_WIDTH - 1
    xs2 = x_sample.reshape(n_s, D)
    row_shapes = ([jax.ShapeDtypeStruct((n_s, D_ATT), F32)] * 3
                  + [jax.ShapeDtypeStruct((LANE_TILES, n_s, LANES), F32)]
                  + [jax.ShapeDtypeStruct((n_s, D), BF16)] * 2)
    q_s, k_s, v_s, ut_s, sa_s, sb_s = pl.pallas_call(
        _inproj_sample_kernel,
        grid=(1,),
        in_specs=[_const_spec((n_s, D))] + inproj_w_specs,
        out_specs=[_const_spec(s.shape) for s in row_shapes],
        out_shape=row_shapes,
        compiler_params=_params("arbitrary"),
        name="inproj_sample",
    )(xs2, nmw, w_in_b, qw, kw)

    seq_spec = lambda *dims: pl.BlockSpec((1,) + dims, lambda b: (b,) + (0,) * len(dims))
    cache_t = [jnp.transpose(c[0], (0, 2, 3, 4, 1)).reshape(SB, 2, GROUP_W, win) for c, win in zip(caches, WINDOWS)]
    win_shapes = [jax.ShapeDtypeStruct((SB, 2, GROUP_W, win), F32) for win in WINDOWS]
    win_specs = [seq_spec(2, GROUP_W, win) for win in WINDOWS]
    o_att_s, kv0_s, kv1_s, kv2_s = pl.pallas_call(
        functools.partial(_sample_step_kernel, n_new=SS),
        grid=(SB,),
        in_specs=[seq_spec(SS, D_ATT)] * 3 + win_specs,
        out_specs=[seq_spec(SS, GROUP_W)] + win_specs,
        out_shape=[jax.ShapeDtypeStruct((SB, SS, GROUP_W), F32)] + win_shapes,
        scratch_shapes=[pltpu.VMEM((3, 8, D_ATT), F32), pltpu.VMEM((GROUP_W, LANES), F32)],
        compiler_params=_params("arbitrary"),
        name="sample_step",
    )(q_s.reshape(SB, SS, D_ATT), k_s.reshape(SB, SS, D_ATT), v_s.reshape(SB, SS, D_ATT), *cache_t)

    state_t = jnp.transpose(state_conv[0], (1, 0, 2))
    y_s, conv_state_t = pl.pallas_call(
        functools.partial(_merge_ffn_sample_kernel, n_new=SS, n_seq=SB),
        grid=(1,),
        in_specs=[_const_spec((n_s, D)), _const_spec((n_s, GROUP_W)), _const_spec((LANE_TILES, n_s, LANES)),
                  _const_spec((n_state, SB, C_CONV)), _const_spec((n_s, D)), _const_spec((n_s, D))]
        + conv_w_specs + merge_w_specs + ffn_w_specs,
        out_specs=[_const_spec((n_s, D)), _const_spec((n_state, SB, C_CONV))],
        out_shape=[jax.ShapeDtypeStruct((n_s, D), F32), jax.ShapeDtypeStruct((n_state, SB, C_CONV), F32)],
        scratch_shapes=[pltpu.VMEM((LANE_TILES, n_s, LANES), F32)],
        compiler_params=_params("arbitrary"),
        name="merge_ffn_sample",
    )(xs2, o_att_s.reshape(n_s, GROUP_W), ut_s, state_t, sa_s, sb_s, cw, cb, w_att_b, lnw, lnb, wco_b, wout_b,
      nfw, wgu_b, wd_b)

    def kv_cols(a, nb, win):
        return a.reshape(nb, 2, HEADS_PER_GROUP, HEAD_DIM, win).transpose(0, 4, 1, 2, 3)[None]

    return (y_p, y_s.reshape(SB, SS, D),
            kv_cols(tails_p[0], B, WINDOWS[0]), kv_cols(kv0_s, SB, WINDOWS[0]),
            kv_cols(tails_p[1], B, WINDOWS[1]), kv_cols(kv1_s, SB, WINDOWS[1]),
            kv_cols(tails_p[2], B, WINDOWS[2]), kv_cols(kv2_s, SB, WINDOWS[2]),
            conv_tail_p[:, HALO - n_state:, :][None], jnp.transpose(conv_state_t, (1, 0, 2))[None])
```

```python
import functools
import math

import jax
import jax.numpy as jnp
from jax import lax
from jax.experimental import pallas as pl
from jax.experimental.pallas import tpu as pltpu

F32 = jnp.float32
BF16 = jnp.bfloat16

D_MODEL = 1024
HEAD_DIM = 64
HEADS_PER_GROUP = 4
GROUP_W = HEADS_PER_GROUP * HEAD_DIM
N_GROUPS = 3
D_ATT = N_GROUPS * GROUP_W
WINDOWS = (128, 512, 2048)
DILATIONS = (1, 4, 16)
BAND = 128
SPAN = 2048
C_CONV = D_MODEL
CONV_WIDTH = 31
HALO = 32
D_FF = 2816
FF_CHUNK = D_FF
N_IN = 3 * D_ATT + 2 * C_CONV + 2 * D_MODEL
RMS_EPS = 1e-6
LN_EPS = 1e-5
NEG_INF = -1e30
ATT_SCALE = HEAD_DIM ** -0.5
Q_SCALE = ATT_SCALE * math.log2(math.e)
LANES = 128
LANE_TILES = C_CONV // LANES
VMEM_LIMIT = 56 * 1024 * 1024

O_Q, O_K, O_V = 0, D_ATT, 2 * D_ATT
O_UA = 3 * D_ATT
O_UB = O_UA + C_CONV
O_GA = O_UB + C_CONV
O_GB = O_GA + D_MODEL


def _dot(a, b):
    return jnp.dot(a, b, preferred_element_type=F32)


def _sigmoid(x):
    return 1.0 / (1.0 + jnp.exp(-x))


def _rms_rows(x, w):
    ms = jnp.mean(x * x, axis=-1, keepdims=True)
    return x * lax.rsqrt(ms + RMS_EPS) * w


def _head_norm_tiles(z, w_ref, scale):
    rows = z.shape[0]
    lane = lax.broadcasted_iota(jnp.int32, (rows, LANES), 1)
    low = lane < HEAD_DIM
    tiles = []
    for c in range(D_ATT // LANES):
        x = z[:, c * LANES:(c + 1) * LANES]
        x2 = x * x
        s_lo = jnp.sum(jnp.where(low, x2, 0.0), axis=-1, keepdims=True)
        s_hi = jnp.sum(jnp.where(low, 0.0, x2), axis=-1, keepdims=True)
        ms = jnp.where(low, s_lo, s_hi) * (1.0 / HEAD_DIM)
        y = x * lax.rsqrt(ms + RMS_EPS) * w_ref[:, c * LANES:(c + 1) * LANES]
        tiles.append(y * scale if scale != 1.0 else y)
    return tiles


def _in_projection(x, nw_ref, w_ref, qw_ref, kw_ref, store_qkv):
    xn = _rms_rows(x, nw_ref[...]).astype(BF16)
    for c, t in enumerate(_head_norm_tiles(_dot(xn, w_ref[:, O_Q:O_Q + D_ATT]), qw_ref, Q_SCALE)):
        store_qkv(0, c, t)
    for c, t in enumerate(_head_norm_tiles(_dot(xn, w_ref[:, O_K:O_K + D_ATT]), kw_ref, 1.0)):
        store_qkv(1, c, t)
    v = _dot(xn, w_ref[:, O_V:O_V + D_ATT])
    for c in range(D_ATT // LANES):
        store_qkv(2, c, v[:, c * LANES:(c + 1) * LANES])
    ua = _dot(xn, w_ref[:, O_UA:O_UA + C_CONV])
    ub = _dot(xn, w_ref[:, O_UB:O_UB + C_CONV])
    u = ua * _sigmoid(ub)
    sa = _sigmoid(_dot(xn, w_ref[:, O_GA:O_GA + D_MODEL]))
    sb = _sigmoid(_dot(xn, w_ref[:, O_GB:O_GB + D_MODEL]))
    return u, sa, sb


def _inproj_prompt_kernel(x_ref, nw_ref, w_ref, qw_ref, kw_ref,
                          qkv0, qkv1, qkv2, u_ref, gates_ref,
                          t0, t1, t2, ct_ref, qs_ref, ks_ref, vs_ref, *, tm):
    stage = (qs_ref, ks_ref, vs_ref)

    def store_qkv(which, c, tile):
        stage[which][c] = tile

    u, sa, sb = _in_projection(x_ref[0], nw_ref, w_ref, qw_ref, kw_ref, store_qkv)
    u_ref[0] = u
    gates_ref[0, :, 0:D_MODEL] = sa.astype(gates_ref.dtype)
    gates_ref[0, :, D_MODEL:2 * D_MODEL] = sb.astype(gates_ref.dtype)
    tiles_per_group = GROUP_W // LANES
    for which, src in enumerate((qs_ref, ks_ref, vs_ref)):
        for g, (dil, out) in enumerate(zip(DILATIONS, (qkv0, qkv1, qkv2))):
            for p in range(tiles_per_group):
                c = g * tiles_per_group + p
                lanes = slice(which * GROUP_W + p * LANES, which * GROUP_W + (p + 1) * LANES)
                if dil == 1:
                    out[0, 0, :, lanes] = src[c].astype(BF16)
                else:
                    for r in range(dil):
                        out[0, r, :, lanes] = src[c, pl.ds(r, tm // dil, stride=dil), :].astype(BF16)
    for g, (win, t_ref) in enumerate(zip(WINDOWS, (t0, t1, t2))):
        rows = min(win, tm)
        for which, src in enumerate((ks_ref, vs_ref)):
            for p in range(tiles_per_group):
                c = g * tiles_per_group + p
                t_ref[0, which, p * LANES:(p + 1) * LANES, :] = src[c, tm - rows:tm, :].T
    ct_ref[0] = u[tm - HALO:tm, :]


def _inproj_sample_kernel(x_ref, nw_ref, w_ref, qw_ref, kw_ref, q_ref, k_ref, v_ref, u_ref, sa_ref, sb_ref):
    outs = (q_ref, k_ref, v_ref)

    def store_qkv(which, c, tile):
        outs[which][:, c * LANES:(c + 1) * LANES] = tile

    u, sa, sb = _in_projection(x_ref[...], nw_ref, w_ref, qw_ref, kw_ref, store_qkv)
    for lt in range(LANE_TILES):
        u_ref[lt] = u[:, lt * LANES:(lt + 1) * LANES]
    sa_ref[...] = sa.astype(sa_ref.dtype)
    sb_ref[...] = sb.astype(sb_ref.dtype)


def _attn_unit(qs, kks, vvs, lo_limits):
    qi = lax.broadcasted_iota(jnp.int32, (2 * BAND, 2 * BAND), 0) & (BAND - 1)
    ki = lax.broadcasted_iota(jnp.int32, (2 * BAND, 2 * BAND), 1)
    band = (ki >= qi) & (ki <= qi + BAND)
    low = lax.broadcasted_iota(jnp.int32, (BAND, LANES), 1) < HEAD_DIM
    scores = []
    for q2, kk in zip(qs, kks):
        zero = jnp.zeros_like(q2)
        q_st = jnp.concatenate([jnp.where(low, q2, zero), jnp.where(low, zero, q2)], axis=0)
        scores.append(lax.dot_general(q_st, kk, (((1,), (1,)), ((), ())), preferred_element_type=F32))
    res = []
    for s, vv, lo in zip(scores, vvs, lo_limits):
        valid = band if isinstance(lo, int) and lo == 0 else band & (ki >= lo)
        s = jnp.where(valid, s, NEG_INF)
        m = jnp.max(s, axis=-1, keepdims=True)
        p = jnp.exp2(s - m)
        l = jnp.sum(p, axis=-1, keepdims=True)
        o = _dot(p.astype(BF16), vv) / l
        lse = m + jnp.log2(l)
        res.append((jnp.where(low, o[0:BAND], o[BAND:2 * BAND]),
                    jnp.where(low, lse[0:BAND], lse[BAND:2 * BAND])))
    return res


def _attn_prompt_kernel(cur0, prev0, cur1, prev1, cur2, prev2, o_ref, og_ref, lg_ref):
    sb = pl.program_id(1)
    refs = ((cur0, prev0), (cur1, prev1), (cur2, prev2))
    first_lo = jnp.where(sb > 0, 0, BAND)
    tiles = GROUP_W // LANES

    for g in range(N_GROUPS):
        dil, win = DILATIONS[g], WINDOWS[g]
        cur_ref, prev_ref = refs[g]

        def units(blocks, g=g, dil=dil, win=win, cur_ref=cur_ref, prev_ref=prev_ref):
            qs, kks, vvs, los, dst = [], [], [], [], []
            for seg, r, first in blocks:
                row0 = seg * BAND
                if not isinstance(row0, int):
                    row0 = pl.multiple_of(row0, BAND)
                start = seg * win + r
                rows = pl.ds(start, BAND) if dil == 1 else pl.ds(start, BAND, stride=dil)
                for p in range(tiles):
                    ql = slice(p * LANES, (p + 1) * LANES)
                    kl = slice(GROUP_W + p * LANES, GROUP_W + (p + 1) * LANES)
                    vl = slice(2 * GROUP_W + p * LANES, 2 * GROUP_W + (p + 1) * LANES)
                    qs.append(cur_ref[0, r, pl.ds(row0, BAND), ql])
                    if first:
                        kks.append(jnp.concatenate([prev_ref[0, r, :, kl], cur_ref[0, r, 0:BAND, kl]], axis=0))
                        vvs.append(jnp.concatenate([prev_ref[0, r, :, vl], cur_ref[0, r, 0:BAND, vl]], axis=0))
                        los.append(first_lo)
                    else:
                        kks.append(cur_ref[0, r, pl.ds(row0 - BAND, 2 * BAND), kl])
                        vvs.append(cur_ref[0, r, pl.ds(row0 - BAND, 2 * BAND), vl])
                        los.append(0)
                    dst.append((g * tiles + p, rows))
            for (slot, rows), (o2, l2) in zip(dst, _attn_unit(qs, kks, vvs, los)):
                og_ref[slot, rows, :] = o2
                lg_ref[slot, rows, :] = l2

        shift = dil.bit_length() - 1
        per = 4
        if dil == 1:
            units([(0, 0, True), (1, 0, False), (2, 0, False), (3, 0, False)])

            def rest_body(t, c, units=units):
                units([(per + per * t + k, 0, False) for k in range(per)])
                return c
            lax.fori_loop(0, (SPAN // win - per) // per, rest_body, 0)
        else:
            def first_body(t, c, units=units):
                units([(0, per * t + k, True) for k in range(per)])
                return c
            lax.fori_loop(0, dil // per, first_body, 0)
            n_rest = (SPAN // win - 1) * dil
            if n_rest:
                def rest_body(t, c, units=units, dil=dil, shift=shift):
                    idx = [per * t + k for k in range(per)]
                    units([(1 + (a >> shift), a & (dil - 1), False) for a in idx])
                    return c
                lax.fori_loop(0, n_rest // per, rest_body, 0)

    chunk = 256

    def merge_body(c, carry):
        rows = pl.ds(pl.multiple_of(c * chunk, chunk), chunk)
        for p in range(tiles):
            l0, l1, l2 = lg_ref[p, rows, :], lg_ref[tiles + p, rows, :], lg_ref[2 * tiles + p, rows, :]
            m = jnp.maximum(jnp.maximum(l0, l1), l2)
            w0, w1, w2 = jnp.exp2(l0 - m), jnp.exp2(l1 - m), jnp.exp2(l2 - m)
            num = (w0 * og_ref[p, rows, :] + w1 * og_ref[tiles + p, rows, :]
                   + w2 * og_ref[2 * tiles + p, rows, :])
            o_ref[0, rows, p * LANES:(p + 1) * LANES] = (num / (w0 + w1 + w2)).astype(o_ref.dtype)
        return carry

    lax.fori_loop(0, SPAN // chunk, merge_body, 0)


def _merge_math(x, o_att, c, sa, sb, watt_ref, lnw_ref, lnb_ref, wco_ref, wout_ref):
    a = _dot(o_att.astype(BF16), watt_ref[...])
    mu = jnp.mean(c, axis=-1, keepdims=True)
    xc = c - mu
    var = jnp.mean(xc * xc, axis=-1, keepdims=True)
    y = xc * lax.rsqrt(var + LN_EPS) * lnw_ref[...] + lnb_ref[...]
    act = y * _sigmoid(y)
    cb = _dot(act.astype(BF16), wco_ref[...])
    h = sa.astype(F32) * a + sb.astype(F32) * cb
    return x + _dot(h.astype(BF16), wout_ref[...])


def _ffn_math(x, nw_ref, wgu_ref, wd_ref):
    xn = _rms_rows(x, nw_ref[...]).astype(BF16)
    acc = x
    for c0 in range(0, D_FF, FF_CHUNK):
        g = _dot(xn, wgu_ref[:, c0:c0 + FF_CHUNK])
        up = _dot(xn, wgu_ref[:, D_FF + c0:D_FF + c0 + FF_CHUNK])
        act = (g * _sigmoid(g) * up).astype(BF16)
        acc = acc + _dot(act, wd_ref[c0:c0 + FF_CHUNK, :])
    return acc


def _merge_ffn_prompt_kernel(x_ref, oa_ref, u_ref, uh_ref, gates_ref, watt_ref, cw_ref, cb_ref,
                             lnw_ref, lnb_ref, wco_ref, wout_ref, nfw_ref, wgu_ref, wd_ref,
                             y_ref, ubuf_ref, cbuf_ref, *, tm, blocks_per_seq):
    n = pl.program_id(0)
    seq_start = lax.rem(n, blocks_per_seq) == 0
    for lt in range(LANE_TILES):
        lanes = slice(lt * LANES, (lt + 1) * LANES)
        ubuf_ref[lt, 0:HALO, :] = jnp.where(seq_start, 0.0, uh_ref[:, lanes])
        ubuf_ref[lt, HALO:HALO + tm, :] = u_ref[:, lanes]
    rc = 64
    off = HALO - (CONV_WIDTH - 1)
    for lt in range(LANE_TILES):
        lanes = slice(lt * LANES, (lt + 1) * LANES)
        for parity in range(2):
            for r0 in range(0, tm // 2, rc):
                acc = jnp.broadcast_to(cb_ref[:, lanes], (rc, LANES))
                for j in range(CONV_WIDTH):
                    start = 2 * r0 + parity + off + j
                    acc = acc + ubuf_ref[lt, pl.ds(start, rc, stride=2), :] * cw_ref[j:j + 1, lanes]
                cbuf_ref[lt, pl.ds(2 * r0 + parity, rc, stride=2), :] = acc
    c = jnp.concatenate([cbuf_ref[lt] for lt in range(LANE_TILES)], axis=1)
    x1 = _merge_math(x_ref[...], oa_ref[...], c, gates_ref[:, 0:D_MODEL], gates_ref[:, D_MODEL:2 * D_MODEL],
                     watt_ref, lnw_ref, lnb_ref, wco_ref, wout_ref)
    y_ref[...] = _ffn_math(x1, nfw_ref, wgu_ref, wd_ref)


def _merge_ffn_sample_kernel(x_ref, oa_ref, ut_ref, st_ref, sa_ref, sb_ref, cw_ref, cb_ref, watt_ref, lnw_ref,
                             lnb_ref, wco_ref, wout_ref, nfw_ref, wgu_ref, wd_ref, y_ref, cs_ref, c_scr,
                             *, n_new, n_seq):
    n_state = CONV_WIDTH - 1

    def ext_rows(k, lt):
        if k < n_state:
            return st_ref[k, :, lt * LANES:(lt + 1) * LANES]
        return ut_ref[lt, pl.ds(k - n_state, n_seq, stride=n_new), :]

    for lt in range(LANE_TILES):
        lanes = slice(lt * LANES, (lt + 1) * LANES)
        for s in range(n_new):
            acc = jnp.broadcast_to(cb_ref[:, lanes], (n_seq, LANES))
            for j in range(CONV_WIDTH):
                acc = acc + ext_rows(s + j, lt) * cw_ref[j:j + 1, lanes]
            c_scr[lt, pl.ds(s, n_seq, stride=n_new), :] = acc
        for k in range(n_state):
            cs_ref[k, :, lanes] = ext_rows(k + n_new, lt)
    c = jnp.concatenate([c_scr[lt] for lt in range(LANE_TILES)], axis=1)
    x1 = _merge_math(x_ref[...], oa_ref[...], c, sa_ref[...], sb_ref[...],
                     watt_ref, lnw_ref, lnb_ref, wco_ref, wout_ref)
    y_ref[...] = _ffn_math(x1, nfw_ref, wgu_ref, wd_ref)


def _shift_in(x, tail, n_new):
    rows, width = x.shape
    lane = lax.broadcasted_iota(jnp.int32, (rows, LANES), 1)
    keep = lane < LANES - n_new
    rolled = [pltpu.roll(x[:, t * LANES:(t + 1) * LANES], LANES - n_new, axis=1) for t in range(width // LANES)]
    rolled.append(tail)
    return jnp.concatenate([jnp.where(keep, rolled[t], rolled[t + 1]) for t in range(width // LANES)], axis=1)


def _sample_step_kernel(q_ref, kn_ref, vn_ref, c0_ref, c1_ref, c2_ref,
                        o_ref, s0_ref, s1_ref, s2_ref, new8_ref, tail_ref, *, n_new):
    caches = (c0_ref, c1_ref, c2_ref)
    outs = (s0_ref, s1_ref, s2_ref)
    pad = 8
    row = lax.broadcasted_iota(jnp.int32, (HEADS_PER_GROUP * pad, GROUP_W), 0)
    col = lax.broadcasted_iota(jnp.int32, (HEADS_PER_GROUP * pad, GROUP_W), 1)
    head_lanes = (col >> 6) == (row >> 3)
    new8_ref[...] = jnp.zeros(new8_ref.shape, F32)
    new8_ref[0, 0:n_new, :] = q_ref[0]
    new8_ref[1, 0:n_new, :] = kn_ref[0]
    new8_ref[2, 0:n_new, :] = vn_ref[0]
    tail_ref[...] = jnp.zeros(tail_ref.shape, F32)
    o_parts, l_parts = [], []
    for g, (win, dil) in enumerate(zip(WINDOWS, DILATIONS)):
        cols = slice(g * GROUP_W, (g + 1) * GROUP_W)
        kt = caches[g][0, 0]
        vt = caches[g][0, 1]
        for which, xt in enumerate((kt, vt)):
            tail_ref[:, 0:pad] = new8_ref[1 + which, :, cols].T
            tail = pltpu.roll(tail_ref[...], LANES - n_new, axis=1)
            outs[g][0, which] = _shift_in(xt, tail, n_new)
        q8 = new8_ref[0, :, cols]
        qm = jnp.where(head_lanes, jnp.concatenate([q8] * HEADS_PER_GROUP, axis=0), 0.0).astype(BF16)
        kn8 = new8_ref[1, :, cols].astype(BF16)
        vn8 = new8_ref[2, :, cols].astype(BF16)
        nt = (((1,), (1,)), ((), ()))
        s_c = _dot(qm, kt.astype(BF16))
        s_n = lax.dot_general(qm, kn8, nt, preferred_element_type=F32)
        sq_c = lax.broadcasted_iota(jnp.int32, s_c.shape, 0) & (pad - 1)
        kc_i = lax.broadcasted_iota(jnp.int32, s_c.shape, 1)
        d_c = win + sq_c - kc_i
        ok_c = ((d_c & (dil - 1)) == 0) & (d_c <= win)
        sq_n = lax.broadcasted_iota(jnp.int32, s_n.shape, 0) & (pad - 1)
        kn_i = lax.broadcasted_iota(jnp.int32, s_n.shape, 1)
        d_n = sq_n - kn_i
        ok_n = (d_n >= 0) & ((d_n & (dil - 1)) == 0) & (kn_i < n_new)
        s_c = jnp.where(ok_c, s_c, NEG_INF)
        s_n = jnp.where(ok_n, s_n, NEG_INF)
        m = jnp.maximum(jnp.max(s_c, axis=-1, keepdims=True), jnp.max(s_n, axis=-1, keepdims=True))
        p_c = jnp.exp2(s_c - m)
        p_n = jnp.exp2(s_n - m)
        l = jnp.sum(p_c, axis=-1, keepdims=True) + jnp.sum(p_n, axis=-1, keepdims=True)
        acc = (lax.dot_general(p_c.astype(BF16), vt.astype(BF16), nt, preferred_element_type=F32)
               + _dot(p_n.astype(BF16), vn8))
        o_parts.append(acc / l)
        l_parts.append(m + jnp.log2(l))
    m = jnp.maximum(jnp.maximum(l_parts[0], l_parts[1]), l_parts[2])
    w = [jnp.exp2(lp - m) for lp in l_parts]
    num = w[0] * o_parts[0] + w[1] * o_parts[1] + w[2] * o_parts[2]
    om = jnp.where(head_lanes, num / (w[0] + w[1] + w[2]), 0.0)
    o8 = om[0:pad] + om[pad:2 * pad] + om[2 * pad:3 * pad] + om[3 * pad:4 * pad]
    o_ref[0] = o8[0:n_new, :]


def _const_spec(shape, single=False):
    nd = len(shape)
    if single:
        return pl.BlockSpec(shape, lambda *_: (0,) * nd, pipeline_mode=pl.Buffered(1))
    return pl.BlockSpec(shape, lambda *_: (0,) * nd)


def _params(*sem):
    return pltpu.CompilerParams(dimension_semantics=sem, vmem_limit_bytes=VMEM_LIMIT)


def kernel(x_prompt, x_sample, cache_kv_w128, cache_kv_w512, cache_kv_w2048, state_conv, norm_mix_w, w_in,
           q_norm_w, k_norm_w, w_att, conv_w, conv_b, conv_ln_w, conv_ln_b, w_conv_out, w_out, norm_ffn_w,
           w_gate_up, w_down):
    B, T, D = x_prompt.shape
    SB, SS, _ = x_sample.shape
    assert D == D_MODEL and T % SPAN == 0 and norm_mix_w.shape[0] == 1
    caches = (cache_kv_w128, cache_kv_w512, cache_kv_w2048)
    for c, win in zip(caches, WINDOWS):
        assert c.shape[2] == win, "cached window shorter than the attention window is not supported"

    nmw = norm_mix_w.reshape(1, D)
    w_in_b = w_in[0].astype(BF16)
    qw = jnp.tile(q_norm_w[0], D_ATT // HEAD_DIM).reshape(1, D_ATT)
    kw = jnp.tile(k_norm_w[0], D_ATT // HEAD_DIM).reshape(1, D_ATT)
    w_att_b = w_att[0].astype(BF16)
    cw = conv_w[0]
    cb = conv_b.reshape(1, C_CONV)
    lnw = conv_ln_w.reshape(1, C_CONV)
    lnb = conv_ln_b.reshape(1, C_CONV)
    wco_b = w_conv_out[0].astype(BF16)
    wout_b = w_out[0].astype(BF16)
    nfw = norm_ffn_w.reshape(1, D)
    wgu_b = w_gate_up[0].astype(BF16)
    wd_b = w_down[0].astype(BF16)

    tm = 256
    n_t = T // tm
    perm_shapes = [jax.ShapeDtypeStruct((B, dil, T // dil, D_ATT), BF16) for dil in DILATIONS]
    perm_specs = [pl.BlockSpec((1, dil, tm // dil, D_ATT), lambda b, i: (b, 0, i, 0)) for dil in DILATIONS]
    tail_rows = [min(win, tm) for win in WINDOWS]
    tail_shapes = [jax.ShapeDtypeStruct((B, 2, GROUP_W, win), F32) for win in WINDOWS]
    tail_specs = [
        pl.BlockSpec((1, 2, GROUP_W, rows),
                     functools.partial(lambda b, i, first: (b, 0, 0, jnp.maximum(i - first, 0)),
                                       first=n_t - win // rows))
        for win, rows in zip(WINDOWS, tail_rows)]
    tok_spec = pl.BlockSpec((1, tm, D), lambda b, i: (b, i, 0))
    inproj_w_specs = [_const_spec((1, D)), _const_spec((D, N_IN), single=True), _const_spec((1, D_ATT)),
                      _const_spec((1, D_ATT))]
    res = pl.pallas_call(
        functools.partial(_inproj_prompt_kernel, tm=tm),
        grid=(B, n_t),
        in_specs=[tok_spec] + inproj_w_specs,
        out_specs=perm_specs + [tok_spec, pl.BlockSpec((1, tm, 2 * D), lambda b, i: (b, i, 0))] + tail_specs
        + [pl.BlockSpec((1, HALO, C_CONV), lambda b, i: (b, 0, 0))],
        out_shape=perm_shapes + [jax.ShapeDtypeStruct((B, T, C_CONV), F32)]
        + [jax.ShapeDtypeStruct((B, T, 2 * D), BF16)] + tail_shapes
        + [jax.ShapeDtypeStruct((B, HALO, C_CONV), F32)],
        scratch_shapes=[pltpu.VMEM((D_ATT // LANES, tm, LANES), F32)] * 3,
        compiler_params=_params("arbitrary", "arbitrary"),
        name="inproj_prompt",
    )(x_prompt, nmw, w_in_b, qw, kw)
    qkv_p = res[0:3]
    u_p, gates_p = res[3:5]
    tails_p = res[5:8]
    conv_tail_p = res[8]

    n_sb = T // SPAN
    att_in, att_specs = [], []
    for g, dil in enumerate(DILATIONS):
        rows = SPAN // dil
        cur = pl.BlockSpec((1, dil, rows, D_ATT), lambda b, s: (b, 0, s, 0))
        prev = pl.BlockSpec((1, dil, BAND, D_ATT),
                            functools.partial(lambda b, s, n: (b, 0, jnp.maximum(s * n - 1, 0), 0),
                                              n=rows // BAND))
        att_in += [qkv_p[g], qkv_p[g]]
        att_specs += [cur, prev]
    o_att_p = pl.pallas_call(
        _attn_prompt_kernel,
        grid=(B, n_sb),
        in_specs=att_specs,
        out_specs=pl.BlockSpec((1, SPAN, GROUP_W), lambda b, s: (b, s, 0)),
        out_shape=jax.ShapeDtypeStruct((B, T, GROUP_W), BF16),
        scratch_shapes=[pltpu.VMEM((N_GROUPS * GROUP_W // LANES, SPAN, LANES), F32)] * 2,
        compiler_params=_params("arbitrary", "arbitrary"),
        name="attn_prompt",
    )(*att_in)

    tm3 = 512
    n_tok = B * T
    tok3 = lambda width: pl.BlockSpec((tm3, width), lambda n: (n, 0))
    merge_w_specs = [_const_spec((GROUP_W, D), single=True), _const_spec((1, C_CONV)), _const_spec((1, C_CONV)),
                     _const_spec((C_CONV, D), single=True), _const_spec((D, D), single=True)]
    conv_w_specs = [_const_spec((CONV_WIDTH, C_CONV)), _const_spec((1, C_CONV))]
    ffn_w_specs = [_const_spec((1, D)), _const_spec((D, 2 * D_FF), single=True),
                   _const_spec((D_FF, D), single=True)]
    u_flat = u_p.reshape(n_tok, C_CONV)
    y_p = pl.pallas_call(
        functools.partial(_merge_ffn_prompt_kernel, tm=tm3, blocks_per_seq=T // tm3),
        grid=(n_tok // tm3,),
        in_specs=[tok3(D), tok3(GROUP_W), tok3(C_CONV),
                  pl.BlockSpec((HALO, C_CONV), lambda n: (jnp.maximum(n * (tm3 // HALO) - 1, 0), 0)),
                  tok3(2 * D), merge_w_specs[0]] + conv_w_specs + merge_w_specs[1:] + ffn_w_specs,
        out_specs=tok3(D),
        out_shape=jax.ShapeDtypeStruct((n_tok, D), F32),
        scratch_shapes=[pltpu.VMEM((LANE_TILES, tm3 + HALO, LANES), F32), pltpu.VMEM((LANE_TILES, tm3, LANES), F32)],
        compiler_params=_params("arbitrary"),
        name="merge_ffn_prompt",
    )(x_prompt.reshape(n_tok, D), o_att_p.reshape(n_tok, GROUP_W), u_flat, u_flat, gates_p.reshape(n_tok, 2 * D),
      w_att_b, cw, cb, lnw, lnb, wco_b, wout_b, nfw, wgu_b, wd_b).reshape(B, T, D)

    n_s = SB * SS
    n_state = CONV_WIDTH - 1
    xs2 = x_sample.reshape(n_s, D)
    row_shapes = ([jax.ShapeDtypeStruct((n_s, D_ATT), F32)] * 3
                  + [jax.ShapeDtypeStruct((LANE_TILES, n_s, LANES), F32)]
                  + [jax.ShapeDtypeStruct((n_s, D), BF16)] * 2)
    q_s, k_s, v_s, ut_s, sa_s, sb_s = pl.pallas_call(
        _inproj_sample_kernel,
        grid=(1,),
        in_specs=[_const_spec((n_s, D))] + inproj_w_specs,
        out_specs=[_const_spec(s.shape) for s in row_shapes],
        out_shape=row_shapes,
        compiler_params=_params("arbitrary"),
        name="inproj_sample",
    )(xs2, nmw, w_in_b, qw, kw)

    seq_spec = lambda *dims: pl.BlockSpec((1,) + dims, lambda b: (b,) + (0,) * len(dims))
    cache_t = [jnp.transpose(c[0], (0, 2, 3, 4, 1)).reshape(SB, 2, GROUP_W, win) for c, win in zip(caches, WINDOWS)]
    win_shapes = [jax.ShapeDtypeStruct((SB, 2, GROUP_W, win), F32) for win in WINDOWS]
    win_specs = [seq_spec(2, GROUP_W, win) for win in WINDOWS]
    o_att_s, kv0_s, kv1_s, kv2_s = pl.pallas_call(
        functools.partial(_sample_step_kernel, n_new=SS),
        grid=(SB,),
        in_specs=[seq_spec(SS, D_ATT)] * 3 + win_specs,
        out_specs=[seq_spec(SS, GROUP_W)] + win_specs,
        out_shape=[jax.ShapeDtypeStruct((SB, SS, GROUP_W), F32)] + win_shapes,
        scratch_shapes=[pltpu.VMEM((3, 8, D_ATT), F32), pltpu.VMEM((GROUP_W, LANES), F32)],
        compiler_params=_params("arbitrary"),
        name="sample_step",
    )(q_s.reshape(SB, SS, D_ATT), k_s.reshape(SB, SS, D_ATT), v_s.reshape(SB, SS, D_ATT), *cache_t)

    state_t = jnp.transpose(state_conv[0], (1, 0, 2))
    y_s, conv_state_t = pl.pallas_call(
        functools.partial(_merge_ffn_sample_kernel, n_new=SS, n_seq=SB),
        grid=(1,),
        in_specs=[_const_spec((n_s, D)), _const_spec((n_s, GROUP_W)), _const_spec((LANE_TILES, n_s, LANES)),
                  _const_spec((n_state, SB, C_CONV)), _const_spec((n_s, D)), _const_spec((n_s, D))]
        + conv_w_specs + merge_w_specs + ffn_w_specs,
        out_specs=[_const_spec((n_s, D)), _const_spec((n_state, SB, C_CONV))],
        out_shape=[jax.ShapeDtypeStruct((n_s, D), F32), jax.ShapeDtypeStruct((n_state, SB, C_CONV), F32)],
        scratch_shapes=[pltpu.VMEM((LANE_TILES, n_s, LANES), F32)],
        compiler_params=_params("arbitrary"),
        name="merge_ffn_sample",
    )(xs2, o_att_s.reshape(n_s, GROUP_W), ut_s, state_t, sa_s, sb_s, cw, cb, w_att_b, lnw, lnb, wco_b, wout_b,
      nfw, wgu_b, wd_b)

    def kv_cols(a, nb, win):
        return a.reshape(nb, 2, HEADS_PER_GROUP, HEAD_DIM, win).transpose(0, 4, 1, 2, 3)[None]

    return (y_p, y_s.reshape(SB, SS, D),
            kv_cols(tails_p[0], B, WINDOWS[0]), kv_cols(kv0_s, SB, WINDOWS[0]),
            kv_cols(tails_p[1], B, WINDOWS[1]), kv_cols(kv1_s, SB, WINDOWS[1]),
            kv_cols(tails_p[2], B, WINDOWS[2]), kv_cols(kv2_s, SB, WINDOWS[2]),
            conv_tail_p[:, HALO - n_state:, :][None], jnp.transpose(conv_state_t, (1, 0, 2))[None])
```

```python
import functools
import math

import jax
import jax.numpy as jnp
from jax import lax
from jax.experimental import pallas as pl
from jax.experimental.pallas import tpu as pltpu

F32 = jnp.float32
BF16 = jnp.bfloat16

D_MODEL = 1024
HEAD_DIM = 64
HEADS_PER_GROUP = 4
GROUP_W = HEADS_PER_GROUP * HEAD_DIM
N_GROUPS = 3
D_ATT = N_GROUPS * GROUP_W
WINDOWS = (128, 512, 2048)
DILATIONS = (1, 4, 16)
BAND = 128
SPAN = 2048
C_CONV = D_MODEL
CONV_WIDTH = 31
HALO = 32
D_FF = 2816
FF_CHUNK = D_FF
N_IN = 3 * D_ATT + 2 * C_CONV + 2 * D_MODEL
RMS_EPS = 1e-6
LN_EPS = 1e-5
NEG_INF = -1e30
ATT_SCALE = HEAD_DIM ** -0.5
Q_SCALE = ATT_SCALE * math.log2(math.e)
LANES = 128
LANE_TILES = C_CONV // LANES
VMEM_LIMIT = 56 * 1024 * 1024

O_Q, O_K, O_V = 0, D_ATT, 2 * D_ATT
O_UA = 3 * D_ATT
O_UB = O_UA + C_CONV
O_GA = O_UB + C_CONV
O_GB = O_GA + D_MODEL


def _dot(a, b):
    return jnp.dot(a, b, preferred_element_type=F32)


def _sigmoid(x):
    return 1.0 / (1.0 + jnp.exp(-x))


def _rms_rows(x, w):
    ms = jnp.mean(x * x, axis=-1, keepdims=True)
    return x * lax.rsqrt(ms + RMS_EPS) * w


def _head_norm_tiles(z, w_ref, scale):
    rows = z.shape[0]
    lane = lax.broadcasted_iota(jnp.int32, (rows, LANES), 1)
    low = lane < HEAD_DIM
    tiles = []
    for c in range(D_ATT // LANES):
        x = z[:, c * LANES:(c + 1) * LANES]
        x2 = x * x
        s_lo = jnp.sum(jnp.where(low, x2, 0.0), axis=-1, keepdims=True)
        s_hi = jnp.sum(jnp.where(low, 0.0, x2), axis=-1, keepdims=True)
        ms = jnp.where(low, s_lo, s_hi) * (1.0 / HEAD_DIM)
        y = x * lax.rsqrt(ms + RMS_EPS) * w_ref[:, c * LANES:(c + 1) * LANES]
        tiles.append(y * scale if scale != 1.0 else y)
    return tiles


def _in_projection(x, nw_ref, w_ref, qw_ref, kw_ref, store_qkv):
    xn = _rms_rows(x, nw_ref[...]).astype(BF16)
    for c, t in enumerate(_head_norm_tiles(_dot(xn, w_ref[:, O_Q:O_Q + D_ATT]), qw_ref, Q_SCALE)):
        store_qkv(0, c, t)
    for c, t in enumerate(_head_norm_tiles(_dot(xn, w_ref[:, O_K:O_K + D_ATT]), kw_ref, 1.0)):
        store_qkv(1, c, t)
    v = _dot(xn, w_ref[:, O_V:O_V + D_ATT])
    for c in range(D_ATT // LANES):
        store_qkv(2, c, v[:, c * LANES:(c + 1) * LANES])
    ua = _dot(xn, w_ref[:, O_UA:O_UA + C_CONV])
    ub = _dot(xn, w_ref[:, O_UB:O_UB + C_CONV])
    u = ua * _sigmoid(ub)
    sa = _sigmoid(_dot(xn, w_ref[:, O_GA:O_GA + D_MODEL]))
    sb = _sigmoid(_dot(xn, w_ref[:, O_GB:O_GB + D_MODEL]))
    return u, sa, sb


def _inproj_prompt_kernel(x_ref, nw_ref, w_ref, qw_ref, kw_ref,
                          qkv0, qkv1, qkv2, u_ref, gates_ref,
                          t0, t1, t2, ct_ref, qs_ref, ks_ref, vs_ref, *, tm):
    stage = (qs_ref, ks_ref, vs_ref)

    def store_qkv(which, c, tile):
        stage[which][c] = tile

    u, sa, sb = _in_projection(x_ref[0], nw_ref, w_ref, qw_ref, kw_ref, store_qkv)
    u_ref[0] = u
    gates_ref[0, :, 0:D_MODEL] = sa.astype(gates_ref.dtype)
    gates_ref[0, :, D_MODEL:2 * D_MODEL] = sb.astype(gates_ref.dtype)
    tiles_per_group = GROUP_W // LANES
    for which, src in enumerate((qs_ref, ks_ref, vs_ref)):
        for g, (dil, out) in enumerate(zip(DILATIONS, (qkv0, qkv1, qkv2))):
            for p in range(tiles_per_group):
                c = g * tiles_per_group + p
                lanes = slice(which * GROUP_W + p * LANES, which * GROUP_W + (p + 1) * LANES)
                if dil == 1:
                    out[0, 0, :, lanes] = src[c].astype(BF16)
                else:
                    for r in range(dil):
                        out[0, r, :, lanes] = src[c, pl.ds(r, tm // dil, stride=dil), :].astype(BF16)
    for g, (win, t_ref) in enumerate(zip(WINDOWS, (t0, t1, t2))):
        rows = min(win, tm)
        for which, src in enumerate((ks_ref, vs_ref)):
            for p in range(tiles_per_group):
                c = g * tiles_per_group + p
                t_ref[0, which, p * LANES:(p + 1) * LANES, :] = src[c, tm - rows:tm, :].T
    ct_ref[0] = u[tm - HALO:tm, :]


def _inproj_sample_kernel(x_ref, nw_ref, w_ref, qw_ref, kw_ref, q_ref, k_ref, v_ref, u_ref, sa_ref, sb_ref):
    outs = (q_ref, k_ref, v_ref)

    def store_qkv(which, c, tile):
        outs[which][:, c * LANES:(c + 1) * LANES] = tile

    u, sa, sb = _in_projection(x_ref[...], nw_ref, w_ref, qw_ref, kw_ref, store_qkv)
    for lt in range(LANE_TILES):
        u_ref[lt] = u[:, lt * LANES:(lt + 1) * LANES]
    sa_ref[...] = sa.astype(sa_ref.dtype)
    sb_ref[...] = sb.astype(sb_ref.dtype)


def _attn_unit(qs, kks, vvs, lo_limits):
    qi = lax.broadcasted_iota(jnp.int32, (2 * BAND, 2 * BAND), 0) & (BAND - 1)
    ki = lax.broadcasted_iota(jnp.int32, (2 * BAND, 2 * BAND), 1)
    band = (ki >= qi) & (ki <= qi + BAND)
    low = lax.broadcasted_iota(jnp.int32, (BAND, LANES), 1) < HEAD_DIM
    scores = []
    for q2, kk in zip(qs, kks):
        zero = jnp.zeros_like(q2)
        q_st = jnp.concatenate([jnp.where(low, q2, zero), jnp.where(low, zero, q2)], axis=0)
        scores.append(lax.dot_general(q_st, kk, (((1,), (1,)), ((), ())), preferred_element_type=F32))
    res = []
    for s, vv, lo in zip(scores, vvs, lo_limits):
        valid = band if isinstance(lo, int) and lo == 0 else band & (ki >= lo)
        s = jnp.where(valid, s, NEG_INF)
        m = jnp.max(s, axis=-1, keepdims=True)
        p = jnp.exp2(s - m)
        l = jnp.sum(p, axis=-1, keepdims=True)
        o = _dot(p.astype(BF16), vv) / l
        lse = m + jnp.log2(l)
        res.append((jnp.where(low, o[0:BAND], o[BAND:2 * BAND]),
                    jnp.where(low, lse[0:BAND], lse[BAND:2 * BAND])))
    return res


def _attn_prompt_kernel(cur0, prev0, cur1, prev1, cur2, prev2, o_ref, og_ref, lg_ref):
    sb = pl.program_id(1)
    refs = ((cur0, prev0), (cur1, prev1), (cur2, prev2))
    first_lo = jnp.where(sb > 0, 0, BAND)
    tiles = GROUP_W // LANES

    for g in range(N_GROUPS):
        dil, win = DILATIONS[g], WINDOWS[g]
        cur_ref, prev_ref = refs[g]

        def units(blocks, g=g, dil=dil, win=win, cur_ref=cur_ref, prev_ref=prev_ref):
            qs, kks, vvs, los, dst = [], [], [], [], []
            for seg, r, first in blocks:
                row0 = seg * BAND
                if not isinstance(row0, int):
                    row0 = pl.multiple_of(row0, BAND)
                start = seg * win + r
                rows = pl.ds(start, BAND) if dil == 1 else pl.ds(start, BAND, stride=dil)
                for p in range(tiles):
                    ql = slice(p * LANES, (p + 1) * LANES)
                    kl = slice(GROUP_W + p * LANES, GROUP_W + (p + 1) * LANES)
                    vl = slice(2 * GROUP_W + p * LANES, 2 * GROUP_W + (p + 1) * LANES)
                    qs.append(cur_ref[0, r, pl.ds(row0, BAND), ql])
                    if first:
                        kks.append(jnp.concatenate([prev_ref[0, r, :, kl], cur_ref[0, r, 0:BAND, kl]], axis=0))
                        vvs.append(jnp.concatenate([prev_ref[0, r, :, vl], cur_ref[0, r, 0:BAND, vl]], axis=0))
                        los.append(first_lo)
                    else:
                        kks.append(cur_ref[0, r, pl.ds(row0 - BAND, 2 * BAND), kl])
                        vvs.append(cur_ref[0, r, pl.ds(row0 - BAND, 2 * BAND), vl])
                        los.append(0)
                    dst.append((g * tiles + p, rows))
            for (slot, rows), (o2, l2) in zip(dst, _attn_unit(qs, kks, vvs, los)):
                og_ref[slot, rows, :] = o2
                lg_ref[slot, rows, :] = l2

        shift = dil.bit_length() - 1
        per = 4
        if dil == 1:
            units([(0, 0, True), (1, 0, False), (2, 0, False), (3, 0, False)])

            def rest_body(t, c, units=units):
                units([(per + per * t + k, 0, False) for k in range(per)])
                return c
            lax.fori_loop(0, (SPAN // win - per) // per, rest_body, 0)
        else:
            def first_body(t, c, units=units):
                units([(0, per * t + k, True) for k in range(per)])
                return c
            lax.fori_loop(0, dil // per, first_body, 0)
            n_rest = (SPAN // win - 1) * dil
            if n_rest:
                def rest_body(t, c, units=units, dil=dil, shift=shift):
                    idx = [per * t + k for k in range(per)]
                    units([(1 + (a >> shift), a & (dil - 1), False) for a in idx])
                    return c
                lax.fori_loop(0, n_rest // per, rest_body, 0)

    chunk = 256

    def merge_body(c, carry):
        rows = pl.ds(pl.multiple_of(c * chunk, chunk), chunk)
        for p in range(tiles):
            l0, l1, l2 = lg_ref[p, rows, :], lg_ref[tiles + p, rows, :], lg_ref[2 * tiles + p, rows, :]
            m = jnp.maximum(jnp.maximum(l0, l1), l2)
            w0, w1, w2 = jnp.exp2(l0 - m), jnp.exp2(l1 - m), jnp.exp2(l2 - m)
            num = (w0 * og_ref[p, rows, :] + w1 * og_ref[tiles + p, rows, :]
                   + w2 * og_ref[2 * tiles + p, rows, :])
            o_ref[0, rows, p * LANES:(p + 1) * LANES] = (num / (w0 + w1 + w2)).astype(o_ref.dtype)
        return carry

    lax.fori_loop(0, SPAN // chunk, merge_body, 0)


def _merge_math(x, o_att, c, sa, sb, watt_ref, lnw_ref, lnb_ref, wco_ref, wout_ref):
    a = _dot(o_att.astype(BF16), watt_ref[...])
    mu = jnp.mean(c, axis=-1, keepdims=True)
    xc = c - mu
    var = jnp.mean(xc * xc, axis=-1, keepdims=True)
    y = xc * lax.rsqrt(var + LN_EPS) * lnw_ref[...] + lnb_ref[...]
    act = y * _sigmoid(y)
    cb = _dot(act.astype(BF16), wco_ref[...])
    h = sa.astype(F32) * a + sb.astype(F32) * cb
    return x + _dot(h.astype(BF16), wout_ref[...])


def _ffn_math(x, nw_ref, wgu_ref, wd_ref):
    xn = _rms_rows(x, nw_ref[...]).astype(BF16)
    acc = x
    for c0 in range(0, D_FF, FF_CHUNK):
        g = _dot(xn, wgu_ref[:, c0:c0 + FF_CHUNK])
        up = _dot(xn, wgu_ref[:, D_FF + c0:D_FF + c0 + FF_CHUNK])
        act = (g * _sigmoid(g) * up).astype(BF16)
        acc = acc + _dot(act, wd_ref[c0:c0 + FF_CHUNK, :])
    return acc


def _merge_ffn_prompt_kernel(x_ref, oa_ref, u_ref, uh_ref, gates_ref, watt_ref, cw_ref, cb_ref,
                             lnw_ref, lnb_ref, wco_ref, wout_ref, nfw_ref, wgu_ref, wd_ref,
                             y_ref, ubuf_ref, cbuf_ref, *, tm, blocks_per_seq):
    n = pl.program_id(0)
    seq_start = lax.rem(n, blocks_per_seq) == 0
    for lt in range(LANE_TILES):
        lanes = slice(lt * LANES, (lt + 1) * LANES)
        ubuf_ref[lt, 0:HALO, :] = jnp.where(seq_start, 0.0, uh_ref[:, lanes])
        ubuf_ref[lt, HALO:HALO + tm, :] = u_ref[:, lanes]
    rc = 64
    off = HALO - (CONV_WIDTH - 1)
    for lt in range(LANE_TILES):
        lanes = slice(lt * LANES, (lt + 1) * LANES)
        for parity in range(2):
            for r0 in range(0, tm // 2, rc):
                acc = jnp.broadcast_to(cb_ref[:, lanes], (rc, LANES))
                for j in range(CONV_WIDTH):
                    start = 2 * r0 + parity + off + j
                    acc = acc + ubuf_ref[lt, pl.ds(start, rc, stride=2), :] * cw_ref[j:j + 1, lanes]
                cbuf_ref[lt, pl.ds(2 * r0 + parity, rc, stride=2), :] = acc
    c = jnp.concatenate([cbuf_ref[lt] for lt in range(LANE_TILES)], axis=1)
    x1 = _merge_math(x_ref[...], oa_ref[...], c, gates_ref[:, 0:D_MODEL], gates_ref[:, D_MODEL:2 * D_MODEL],
                     watt_ref, lnw_ref, lnb_ref, wco_ref, wout_ref)
    y_ref[...] = _ffn_math(x1, nfw_ref, wgu_ref, wd_ref)


def _merge_ffn_sample_kernel(x_ref, oa_ref, ut_ref, st_ref, sa_ref, sb_ref, cw_ref, cb_ref, watt_ref, lnw_ref,
                             lnb_ref, wco_ref, wout_ref, nfw_ref, wgu_ref, wd_ref, y_ref, cs_ref, c_scr,
                             *, n_new, n_seq):
    n_state = CONV_WIDTH - 1

    def ext_rows(k, lt):
        if k < n_state:
            return st_ref[k, :, lt * LANES:(lt + 1) * LANES]
        return ut_ref[lt, pl.ds(k - n_state, n_seq, stride=n_new), :]

    for lt in range(LANE_TILES):
        lanes = slice(lt * LANES, (lt + 1) * LANES)
        for s in range(n_new):
            acc = jnp.broadcast_to(cb_ref[:, lanes], (n_seq, LANES))
            for j in range(CONV_WIDTH):
                acc = acc + ext_rows(s + j, lt) * cw_ref[j:j + 1, lanes]
            c_scr[lt, pl.ds(s, n_seq, stride=n_new), :] = acc
        for k in range(n_state):
            cs_ref[k, :, lanes] = ext_rows(k + n_new, lt)
    c = jnp.concatenate([c_scr[lt] for lt in range(LANE_TILES)], axis=1)
    x1 = _merge_math(x_ref[...], oa_ref[...], c, sa_ref[...], sb_ref[...],
                     watt_ref, lnw_ref, lnb_ref, wco_ref, wout_ref)
    y_ref[...] = _ffn_math(x1, nfw_ref, wgu_ref, wd_ref)


def _shift_in(x, tail, n_new):
    rows, width = x.shape
    lane = lax.broadcasted_iota(jnp.int32, (rows, LANES), 1)
    keep = lane < LANES - n_new
    rolled = [pltpu.roll(x[:, t * LANES:(t + 1) * LANES], LANES - n_new, axis=1) for t in range(width // LANES)]
    rolled.append(tail)
    return jnp.concatenate([jnp.where(keep, rolled[t], rolled[t + 1]) for t in range(width // LANES)], axis=1)


def _sample_step_kernel(q_ref, kn_ref, vn_ref, c0_ref, c1_ref, c2_ref,
                        o_ref, s0_ref, s1_ref, s2_ref, new8_ref, tail_ref, *, n_new):
    caches = (c0_ref, c1_ref, c2_ref)
    outs = (s0_ref, s1_ref, s2_ref)
    pad = 8
    row = lax.broadcasted_iota(jnp.int32, (HEADS_PER_GROUP * pad, GROUP_W), 0)
    col = lax.broadcasted_iota(jnp.int32, (HEADS_PER_GROUP * pad, GROUP_W), 1)
    head_lanes = (col >> 6) == (row >> 3)
    new8_ref[...] = jnp.zeros(new8_ref.shape, F32)
    new8_ref[0, 0:n_new, :] = q_ref[0]
    new8_ref[1, 0:n_new, :] = kn_ref[0]
    new8_ref[2, 0:n_new, :] = vn_ref[0]
    tail_ref[...] = jnp.zeros(tail_ref.shape, F32)
    o_parts, l_parts = [], []
    for g, (win, dil) in enumerate(zip(WINDOWS, DILATIONS)):
        cols = slice(g * GROUP_W, (g + 1) * GROUP_W)
        kt = caches[g][0, 0]
        vt = caches[g][0, 1]
        for which, xt in enumerate((kt, vt)):
            tail_ref[:, 0:pad] = new8_ref[1 + which, :, cols].T
            tail = pltpu.roll(tail_ref[...], LANES - n_new, axis=1)
            outs[g][0, which] = _shift_in(xt, tail, n_new)
        q8 = new8_ref[0, :, cols]
        qm = jnp.where(head_lanes, jnp.concatenate([q8] * HEADS_PER_GROUP, axis=0), 0.0).astype(BF16)
        kn8 = new8_ref[1, :, cols].astype(BF16)
        vn8 = new8_ref[2, :, cols].astype(BF16)
        nt = (((1,), (1,)), ((), ()))
        s_c = _dot(qm, kt.astype(BF16))
        s_n = lax.dot_general(qm, kn8, nt, preferred_element_type=F32)
        sq_c = lax.broadcasted_iota(jnp.int32, s_c.shape, 0) & (pad - 1)
        kc_i = lax.broadcasted_iota(jnp.int32, s_c.shape, 1)
        d_c = win + sq_c - kc_i
        ok_c = ((d_c & (dil - 1)) == 0) & (d_c <= win)
        sq_n = lax.broadcasted_iota(jnp.int32, s_n.shape, 0) & (pad - 1)
        kn_i = lax.broadcasted_iota(jnp.int32, s_n.shape, 1)
        d_n = sq_n - kn_i
        ok_n = (d_n >= 0) & ((d_n & (dil - 1)) == 0) & (kn_i < n_new)
        s_c = jnp.where(ok_c, s_c, NEG_INF)
        s_n = jnp.where(ok_n, s_n, NEG_INF)
        m = jnp.maximum(jnp.max(s_c, axis=-1, keepdims=True), jnp.max(s_n, axis=-1, keepdims=True))
        p_c = jnp.exp2(s_c - m)
        p_n = jnp.exp2(s_n - m)
        l = jnp.sum(p_c, axis=-1, keepdims=True) + jnp.sum(p_n, axis=-1, keepdims=True)
        acc = (lax.dot_general(p_c.astype(BF16), vt.astype(BF16), nt, preferred_element_type=F32)
               + _dot(p_n.astype(BF16), vn8))
        o_parts.append(acc / l)
        l_parts.append(m + jnp.log2(l))
    m = jnp.maximum(jnp.maximum(l_parts[0], l_parts[1]), l_parts[2])
    w = [jnp.exp2(lp - m) for lp in l_parts]
    num = w[0] * o_parts[0] + w[1] * o_parts[1] + w[2] * o_parts[2]
    om = jnp.where(head_lanes, num / (w[0] + w[1] + w[2]), 0.0)
    o8 = om[0:pad] + om[pad:2 * pad] + om[2 * pad:3 * pad] + om[3 * pad:4 * pad]
    o_ref[0] = o8[0:n_new, :]


def _const_spec(shape, single=False):
    nd = len(shape)
    if single:
        return pl.BlockSpec(shape, lambda *_: (0,) * nd, pipeline_mode=pl.Buffered(1))
    return pl.BlockSpec(shape, lambda *_: (0,) * nd)


def _params(*sem):
    return pltpu.CompilerParams(dimension_semantics=sem, vmem_limit_bytes=VMEM_LIMIT)


def kernel(x_prompt, x_sample, cache_kv_w128, cache_kv_w512, cache_kv_w2048, state_conv, norm_mix_w, w_in,
           q_norm_w, k_norm_w, w_att, conv_w, conv_b, conv_ln_w, conv_ln_b, w_conv_out, w_out, norm_ffn_w,
           w_gate_up, w_down):
    B, T, D = x_prompt.shape
    SB, SS, _ = x_sample.shape
    assert D == D_MODEL and T % SPAN == 0 and norm_mix_w.shape[0] == 1
    caches = (cache_kv_w128, cache_kv_w512, cache_kv_w2048)
    for c, win in zip(caches, WINDOWS):
        assert c.shape[2] == win, "cached window shorter than the attention window is not supported"

    nmw = norm_mix_w.reshape(1, D)
    w_in_b = w_in[0].astype(BF16)
    qw = jnp.tile(q_norm_w[0], D_ATT // HEAD_DIM).reshape(1, D_ATT)
    kw = jnp.tile(k_norm_w[0], D_ATT // HEAD_DIM).reshape(1, D_ATT)
    w_att_b = w_att[0].astype(BF16)
    cw = conv_w[0]
    cb = conv_b.reshape(1, C_CONV)
    lnw = conv_ln_w.reshape(1, C_CONV)
    lnb = conv_ln_b.reshape(1, C_CONV)
    wco_b = w_conv_out[0].astype(BF16)
    wout_b = w_out[0].astype(BF16)
    nfw = norm_ffn_w.reshape(1, D)
    wgu_b = w_gate_up[0].astype(BF16)
    wd_b = w_down[0].astype(BF16)

    tm = 512
    n_t = T // tm
    perm_shapes = [jax.ShapeDtypeStruct((B, dil, T // dil, D_ATT), BF16) for dil in DILATIONS]
    perm_specs = [pl.BlockSpec((1, dil, tm // dil, D_ATT), lambda b, i: (b, 0, i, 0)) for dil in DILATIONS]
    tail_rows = [min(win, tm) for win in WINDOWS]
    tail_shapes = [jax.ShapeDtypeStruct((B, 2, GROUP_W, win), F32) for win in WINDOWS]
    tail_specs = [
        pl.BlockSpec((1, 2, GROUP_W, rows),
                     functools.partial(lambda b, i, first: (b, 0, 0, jnp.maximum(i - first, 0)),
                                       first=n_t - win // rows))
        for win, rows in zip(WINDOWS, tail_rows)]
    tok_spec = pl.BlockSpec((1, tm, D), lambda b, i: (b, i, 0))
    inproj_w_specs = [_const_spec((1, D)), _const_spec((D, N_IN), single=True), _const_spec((1, D_ATT)),
                      _const_spec((1, D_ATT))]
    res = pl.pallas_call(
        functools.partial(_inproj_prompt_kernel, tm=tm),
        grid=(B, n_t),
        in_specs=[tok_spec] + inproj_w_specs,
        out_specs=perm_specs + [tok_spec, pl.BlockSpec((1, tm, 2 * D), lambda b, i: (b, i, 0))] + tail_specs
        + [pl.BlockSpec((1, HALO, C_CONV), lambda b, i: (b, 0, 0))],
        out_shape=perm_shapes + [jax.ShapeDtypeStruct((B, T, C_CONV), F32)]
        + [jax.ShapeDtypeStruct((B, T, 2 * D), BF16)] + tail_shapes
        + [jax.ShapeDtypeStruct((B, HALO, C_CONV), F32)],
        scratch_shapes=[pltpu.VMEM((D_ATT // LANES, tm, LANES), F32)] * 3,
        compiler_params=_params("arbitrary", "arbitrary"),
        name="inproj_prompt",
    )(x_prompt, nmw, w_in_b, qw, kw)
    qkv_p = res[0:3]
    u_p, gates_p = res[3:5]
    tails_p = res[5:8]
    conv_tail_p = res[8]

    n_sb = T // SPAN
    att_in, att_specs = [], []
    for g, dil in enumerate(DILATIONS):
        rows = SPAN // dil
        cur = pl.BlockSpec((1, dil, rows, D_ATT), lambda b, s: (b, 0, s, 0))
        prev = pl.BlockSpec((1, dil, BAND, D_ATT),
                            functools.partial(lambda b, s, n: (b, 0, jnp.maximum(s * n - 1, 0), 0),
                                              n=rows // BAND))
        att_in += [qkv_p[g], qkv_p[g]]
        att_specs += [cur, prev]
    o_att_p = pl.pallas_call(
        _attn_prompt_kernel,
        grid=(B, n_sb),
        in_specs=att_specs,
        out_specs=pl.BlockSpec((1, SPAN, GROUP_W), lambda b, s: (b, s, 0)),
        out_shape=jax.ShapeDtypeStruct((B, T, GROUP_W), BF16),
        scratch_shapes=[pltpu.VMEM((N_GROUPS * GROUP_W // LANES, SPAN, LANES), F32)] * 2,
        compiler_params=_params("arbitrary", "arbitrary"),
        name="attn_prompt",
    )(*att_in)

    tm3 = 512
    n_tok = B * T
    tok3 = lambda width: pl.BlockSpec((tm3, width), lambda n: (n, 0))
    merge_w_specs = [_const_spec((GROUP_W, D), single=True), _const_spec((1, C_CONV)), _const_spec((1, C_CONV)),
                     _const_spec((C_CONV, D), single=True), _const_spec((D, D), single=True)]
    conv_w_specs = [_const_spec((CONV_WIDTH, C_CONV)), _const_spec((1, C_CONV))]
    ffn_w_specs = [_const_spec((1, D)), _const_spec((D, 2 * D_FF), single=True),
                   _const_spec((D_FF, D), single=True)]
    u_flat = u_p.reshape(n_tok, C_CONV)
    y_p = pl.pallas_call(
        functools.partial(_merge_ffn_prompt_kernel, tm=tm3, blocks_per_seq=T // tm3),
        grid=(n_tok // tm3,),
        in_specs=[tok3(D), tok3(GROUP_W), tok3(C_CONV),
                  pl.BlockSpec((HALO, C_CONV), lambda n: (jnp.maximum(n * (tm3 // HALO) - 1, 0), 0)),
                  tok3(2 * D), merge_w_specs[0]] + conv_w_specs + merge_w_specs[1:] + ffn_w_specs,
        out_specs=tok3(D),
        out_shape=jax.ShapeDtypeStruct((n_tok, D), F32),
        scratch_shapes=[pltpu.VMEM((LANE_TILES, tm3 + HALO, LANES), F32), pltpu.VMEM((LANE_TILES, tm3, LANES), F32)],
        compiler_params=_params("arbitrary"),
        name="merge_ffn_prompt",
    )(x_prompt.reshape(n_tok, D), o_att_p.reshape(n_tok, GROUP_W), u_flat, u_flat, gates_p.reshape(n_tok, 2 * D),
      w_att_b, cw, cb, lnw, lnb, wco_b, wout_b, nfw, wgu_b, wd_b).reshape(B, T, D)

    n_s = SB * SS
    n_state = CONV_WIDTH - 1
    xs2 = x_sample.reshape(n_s, D)
    row_shapes = ([jax.ShapeDtypeStruct((n_s, D_ATT), F32)] * 3
                  + [jax.ShapeDtypeStruct((LANE_TILES, n_s, LANES), F32)]
                  + [jax.ShapeDtypeStruct((n_s, D), BF16)] * 2)
    q_s, k_s, v_s, ut_s, sa_s, sb_s = pl.pallas_call(
        _inproj_sample_kernel,
        grid=(1,),
        in_specs=[_const_spec((n_s, D))] + inproj_w_specs,
        out_specs=[_const_spec(s.shape) for s in row_shapes],
        out_shape=row_shapes,
        compiler_params=_params("arbitrary"),
        name="inproj_sample",
    )(xs2, nmw, w_in_b, qw, kw)

    seq_spec = lambda *dims: pl.BlockSpec((1,) + dims, lambda b: (b,) + (0,) * len(dims))
    cache_t = [jnp.transpose(c[0], (0, 2, 3, 4, 1)).reshape(SB, 2, GROUP_W, win) for c, win in zip(caches, WINDOWS)]
    win_shapes = [jax.ShapeDtypeStruct((SB, 2, GROUP_W, win), F32) for win in WINDOWS]
    win_specs = [seq_spec(2, GROUP_W, win) for win in WINDOWS]
    o_att_s, kv0_s, kv1_s, kv2_s = pl.pallas_call(
        functools.partial(_sample_step_kernel, n_new=SS),
        grid=(SB,),
        in_specs=[seq_spec(SS, D_ATT)] * 3 + win_specs,
        out_specs=[seq_spec(SS, GROUP_W)] + win_specs,
        out_shape=[jax.ShapeDtypeStruct((SB, SS, GROUP_W), F32)] + win_shapes,
        scratch_shapes=[pltpu.VMEM((3, 8, D_ATT), F32), pltpu.VMEM((GROUP_W, LANES), F32)],
        compiler_params=_params("arbitrary"),
        name="sample_step",
    )(q_s.reshape(SB, SS, D_ATT), k_s.reshape(SB, SS, D_ATT), v_s.reshape(SB, SS, D_ATT), *cache_t)

    state_t = jnp.transpose(state_conv[0], (1, 0, 2))
    y_s, conv_state_t = pl.pallas_call(
        functools.partial(_merge_ffn_sample_kernel, n_new=SS, n_seq=SB),
        grid=(1,),
        in_specs=[_const_spec((n_s, D)), _const_spec((n_s, GROUP_W)), _const_spec((LANE_TILES, n_s, LANES)),
                  _const_spec((n_state, SB, C_CONV)), _const_spec((n_s, D)), _const_spec((n_s, D))]
        + conv_w_specs + merge_w_specs + ffn_w_specs,
        out_specs=[_const_spec((n_s, D)), _const_spec((n_state, SB, C_CONV))],
        out_shape=[jax.ShapeDtypeStruct((n_s, D), F32), jax.ShapeDtypeStruct((n_state, SB, C_CONV), F32)],
        scratch_shapes=[pltpu.VMEM((LANE_TILES, n_s, LANES), F32)],
        compiler_params=_params("arbitrary"),
        name="merge_ffn_sample",
    )(xs2, o_att_s.reshape(n_s, GROUP_W), ut_s, state_t, sa_s, sb_s, cw, cb, w_att_b, lnw, lnb, wco_b, wout_b,
      nfw, wgu_b, wd_b)

    def kv_cols(a, nb, win):
        return a.reshape(nb, 2, HEADS_PER_GROUP, HEAD_DIM, win).transpose(0, 4, 1, 2, 3)[None]

    return (y_p, y_s.reshape(SB, SS, D),
            kv_cols(tails_p[0], B, WINDOWS[0]), kv_cols(kv0_s, SB, WINDOWS[0]),
            kv_cols(tails_p[1], B, WINDOWS[1]), kv_cols(kv1_s, SB, WINDOWS[1]),
            kv_cols(tails_p[2], B, WINDOWS[2]), kv_cols(kv2_s, SB, WINDOWS[2]),
            conv_tail_p[:, HALO - n_state:, :][None], jnp.transpose(conv_state_t, (1, 0, 2))[None])
```
